```python
import math
import jax, jax.numpy as jnp
from jax import lax
import numpy as np


D_MODEL = 1024
BATCH = 2
SEQ = 8192
DEPTH = 2

N_META = 16
CHUNK = 128
PAD = CHUNK
ROPE_THETA = 10000.0
EPS = 1e-6
LB_FLOOR = 1e-30
RET_HEADS = 4
RET_DK = 128
RET_DV = 256
HG_HEADS = 8
HG_DK = 128
HG_DV = 128
DA_HEADS = 8
DA_DH = 64
DA_DV = 2 * DA_DH
N_BRANCH = 3
BRANCH_WIDTH = 1024
D_FF = 2816
CONV_W = 3
Q_BLOCK = 128
MASK_VALUE = -1e30
RET_QK_W = RET_HEADS * RET_DK
RET_V_W = RET_HEADS * RET_DV
HG_K_W = HG_HEADS * HG_DK
HG_V_W = HG_HEADS * HG_DV
DA_QK_W = DA_HEADS * 2 * DA_DH
DA_V_W = DA_HEADS * DA_DV
IN_SPLITS = (RET_QK_W, RET_QK_W, RET_V_W, RET_V_W, HG_K_W, HG_K_W, HG_V_W, HG_V_W, DA_QK_W, DA_QK_W, DA_V_W, N_BRANCH * D_MODEL)
IN_WIDTH = sum(IN_SPLITS)
F32 = jnp.float32

kernel_name = 'hybrid_retention_hgrn2_diffattn_block'


def _rms(x):
    xf = x.astype(F32)
    return xf * lax.rsqrt(jnp.mean(xf * xf, axis=-1, keepdims=True) + EPS)


def rms_norm(x, g):
    return (_rms(x) * g.astype(F32)).astype(x.dtype)


def rope(x, pos):
    d = x.shape[-1]
    inv = ROPE_THETA ** (-jnp.arange(0, d, 2, dtype=F32) / d)
    ang = pos.astype(F32)[:, None] * inv[None, :]
    cos = jnp.cos(ang)[None, :, None, :]
    sin = jnp.sin(ang)[None, :, None, :]
    x1, x2 = jnp.split(x, 2, axis=-1)
    return jnp.concatenate([x1 * cos - x2 * sin, x2 * cos + x1 * sin], axis=-1)


def to_chunks(x):
    b, l = x.shape[:2]
    return jnp.swapaxes(x.reshape((b, l // CHUNK, CHUNK) + x.shape[2:]), 0, 1)


def from_chunks(x):
    n, b = x.shape[:2]
    return jnp.swapaxes(x, 0, 1).reshape((b, n * CHUNK) + x.shape[3:])


def retention(q, k, v, gate, valid, pos):
    b = q.shape[0]
    q = rope(q, pos)
    k = jnp.where(valid[None, :, None, None], rope(k, pos) * RET_DK ** -0.5, 0.0)
    log_g = jnp.log1p(-jnp.exp2(-5.0 - jnp.arange(RET_HEADS, dtype=F32)))
    idx = jnp.arange(CHUNK, dtype=F32)
    gap = idx[:, None] - idx[None, :]
    intra = jnp.where(gap >= 0, jnp.exp(log_g[:, None, None] * jnp.maximum(gap, 0.0)), 0.0)
    q_dec = jnp.exp(log_g[:, None] * (idx[None, :] + 1.0))
    k_dec = jnp.exp(log_g[:, None] * (CHUNK - 1.0 - idx[None, :]))
    c_dec = jnp.exp(log_g * CHUNK)[None, :, None, None]

    def step(state, xs):
        qc, kc, vc = xs
        s = jnp.einsum('bqhd,bkhd->bhqk', qc, kc) * intra[None]
        o = jnp.einsum('bhqk,bkhe->bqhe', s, vc) + jnp.einsum('bqhd,hq,bhde->bqhe', qc, q_dec, state)
        state = c_dec * state + jnp.einsum('bkhd,hk,bkhe->bhde', kc, k_dec, vc)
        return state, o

    s0 = jnp.zeros((b, RET_HEADS, RET_DK, RET_DV), F32)
    _, o = lax.scan(step, s0, (to_chunks(q), to_chunks(k), to_chunks(v)))
    o = _rms(from_chunks(o)) * jax.nn.silu(gate)
    return o.reshape(b, -1, RET_V_W)


def hgrn2(q, f_logit, inp, gate, lb, valid):
    b = q.shape[0]
    lbh = lb.reshape(HG_HEADS, HG_DK)
    log_f = jnp.logaddexp(jnp.log(jnp.maximum(lbh, LB_FLOOR)), jnp.log1p(-lbh) + jax.nn.log_sigmoid(f_logit))
    k = (1.0 - lbh) * jax.nn.sigmoid(-f_logit)
    v = jnp.where(valid[None, :, None, None], inp, 0.0)
    causal = jnp.tril(jnp.ones((CHUNK, CHUNK), bool))[None, :, :, None, None]

    def step(state, xs):
        qc, kc, vc, lfc = xs
        cb = jnp.cumsum(lfc, axis=1)
        rel = cb[:, :, None] - cb[:, None, :]
        dec = jnp.where(causal, jnp.exp(jnp.where(causal, rel, 0.0)), 0.0)
        a = jnp.einsum('bqhd,bkhd,bqkhd->bhqk', qc, kc, dec)
        o = jnp.einsum('bhqk,bkhe->bqhe', a, vc) + jnp.einsum('bqhd,bhde->bqhe', qc * jnp.exp(cb), state)
        c_end = cb[:, -1]
        state = jnp.exp(c_end)[..., None] * state + jnp.einsum('bkhd,bkhe->bhde', kc * jnp.exp(c_end[:, None] - cb), vc)
        return state, o

    s0 = jnp.zeros((b, HG_HEADS, HG_DK, HG_DV), F32)
    _, o = lax.scan(step, s0, (to_chunks(q), to_chunks(k), to_chunks(v), to_chunks(log_f)))
    o = _rms(from_chunks(o)) * jax.nn.silu(gate)
    return o.reshape(b, -1, HG_V_W)


def diff_attention(q, k, v, lam_p, subln_g, lambda_init, valid, pos):
    b, l = q.shape[:2]
    lp = lam_p.astype(F32)
    lam = jnp.exp(jnp.sum(lp[0] * lp[1])) - jnp.exp(jnp.sum(lp[2] * lp[3])) + lambda_init
    q = rope(q.reshape(b, l, 2 * DA_HEADS, DA_DH), pos).reshape(b, l, DA_HEADS, 2, DA_DH) * DA_DH ** -0.5
    k = rope(k.reshape(b, l, 2 * DA_HEADS, DA_DH), pos).reshape(b, l, DA_HEADS, 2, DA_DH)
    n_blk = l // Q_BLOCK
    q_blocks = jnp.swapaxes(q.reshape(b, n_blk, Q_BLOCK, DA_HEADS, 2, DA_DH), 0, 1)
    key_pos = jnp.arange(l)

    def block(args):
        qb, bi = args
        s = jnp.einsum('bqhmd,bkhmd->bhmqk', qb, k)
        q_pos = bi * Q_BLOCK + jnp.arange(Q_BLOCK)
        allowed = (key_pos[None, :] <= q_pos[:, None]) & valid[None, :]
        p = jax.nn.softmax(jnp.where(allowed, s, MASK_VALUE), axis=-1)
        w = p[:, :, 0] - lam * p[:, :, 1]
        return jnp.einsum('bhqk,bkhe->bqhe', w, v)

    o = lax.map(block, (q_blocks, jnp.arange(n_blk)))
    o = jnp.swapaxes(o, 0, 1).reshape(b, l, DA_HEADS, DA_DV)
    o = _rms(o) * subln_g.astype(F32) * (1.0 - lambda_init)
    return o.reshape(b, l, DA_V_W)


def mixer(hn, w_in, w_branch, w_out, lb, lam_p, subln_g, lambda_init, valid, pos):
    b, l, _ = hn.shape
    proj = (hn @ w_in).astype(F32)
    offs = [int(o) for o in np.cumsum(IN_SPLITS)[:-1]]
    rq, rk, rv, rg, hq, hf, hi, hg, dq, dk, dv, mg = jnp.split(proj, offs, axis=-1)
    o_ret = retention(rq.reshape(b, l, RET_HEADS, RET_DK), rk.reshape(b, l, RET_HEADS, RET_DK),
                      rv.reshape(b, l, RET_HEADS, RET_DV), rg.reshape(b, l, RET_HEADS, RET_DV), valid, pos)
    o_hg = hgrn2(hq.reshape(b, l, HG_HEADS, HG_DK), hf.reshape(b, l, HG_HEADS, HG_DK),
                 hi.reshape(b, l, HG_HEADS, HG_DV), hg.reshape(b, l, HG_HEADS, HG_DV), lb, valid)
    o_da = diff_attention(dq.reshape(b, l, DA_HEADS, 2, DA_DH), dk.reshape(b, l, DA_HEADS, 2, DA_DH),
                          dv.reshape(b, l, DA_HEADS, DA_DV), lam_p, subln_g, lambda_init, valid, pos)
    branches = jnp.stack([o_ret, o_hg, o_da], axis=2)
    y_b = jnp.einsum('blnc,ncd->blnd', branches, w_branch.astype(F32))
    gates = jax.nn.sigmoid(mg.reshape(b, l, N_BRANCH, D_MODEL))
    y = jnp.sum(gates * y_b, axis=2)
    return y.astype(hn.dtype) @ w_out


def conv_ffn(hn, w_ffn_in, conv_w, conv_b, w_ffn_out, valid):
    l = hn.shape[1]
    u = jnp.where(valid[None, :, None], hn @ w_ffn_in, 0.0)
    up = jnp.pad(u, ((0, 0), (CONV_W - 1, 0), (0, 0)))
    c = conv_b + sum(conv_w[j] * up[:, j:j + l] for j in range(CONV_W))
    gate, val = jnp.split(c, 2, axis=-1)
    return (jax.nn.silu(gate) * val) @ w_ffn_out


def setup_inputs(seed: int = 0) -> dict:
    key = jax.random.key(seed)
    ks = jax.random.split(key, 16)

    def nrm(k, shape, scale):
        return jax.random.normal(k, shape, F32) * scale

    return {
        'x': nrm(ks[0], (BATCH, SEQ, D_MODEL), 1.0),
        'meta': nrm(ks[1], (N_META, D_MODEL), 1.0),
        'norm_mix_g': 1.0 + nrm(ks[2], (DEPTH, D_MODEL), 0.02),
        'w_in': nrm(ks[3], (DEPTH, D_MODEL, IN_WIDTH), D_MODEL ** -0.5),
        'w_branch': nrm(ks[4], (DEPTH, N_BRANCH, BRANCH_WIDTH, D_MODEL), BRANCH_WIDTH ** -0.5),
        'w_out': nrm(ks[5], (DEPTH, D_MODEL, D_MODEL), D_MODEL ** -0.5),
        'hg_lb': nrm(ks[6], (DEPTH, HG_K_W), 0.1),
        'da_lambda': nrm(ks[7], (DEPTH, 4, DA_DH), 0.1),
        'da_subln_g': 1.0 + nrm(ks[8], (DEPTH, DA_DV), 0.02),
        'norm_ffn_g': 1.0 + nrm(ks[9], (DEPTH, D_MODEL), 0.02),
        'w_ffn_in': nrm(ks[10], (DEPTH, D_MODEL, 2 * D_FF), D_MODEL ** -0.5),
        'ffn_conv_w': nrm(ks[11], (DEPTH, CONV_W, 2 * D_FF), CONV_W ** -0.5),
        'ffn_conv_b': nrm(ks[12], (DEPTH, 2 * D_FF), 0.01),
        'w_ffn_out': nrm(ks[13], (DEPTH, D_FF, D_MODEL), D_FF ** -0.5),
        'norm_final_g': 1.0 + nrm(ks[14], (D_MODEL,), 0.02),
    }


def reference(x, meta, norm_mix_g, w_in, w_branch, w_out, hg_lb, da_lambda, da_subln_g,
              norm_ffn_g, w_ffn_in, ffn_conv_w, ffn_conv_b, w_ffn_out, norm_final_g):
    b, s, d = x.shape
    l = PAD + s
    h = jnp.concatenate([jnp.zeros((b, PAD - N_META, d), x.dtype),
                         jnp.broadcast_to(meta[None].astype(x.dtype), (b, N_META, d)), x], axis=1)
    t = jnp.arange(l)
    valid = t >= PAD - N_META
    pos = t - (PAD - N_META)
    lb_soft = jax.nn.softmax(hg_lb.astype(F32), axis=0)
    lbs = jnp.cumsum(lb_soft, axis=0) - lb_soft[0]
    for li in range(DEPTH):
        lambda_init = 0.8 - 0.6 * math.exp(-0.3 * li)
        h = h + mixer(rms_norm(h, norm_mix_g[li]), w_in[li], w_branch[li], w_out[li], lbs[li],
                      da_lambda[li], da_subln_g[li], lambda_init, valid, pos)
        h = h + conv_ffn(rms_norm(h, norm_ffn_g[li]), w_ffn_in[li], ffn_conv_w[li], ffn_conv_b[li],
                         w_ffn_out[li], valid)
    h = rms_norm(h, norm_final_g)
    return h[:, PAD:]
```

```python
import functools
import math

import jax
import jax.numpy as jnp
from jax import lax
from jax.experimental import pallas as pl
from jax.experimental.pallas import tpu as pltpu

D_MODEL = 1024
DEPTH = 2
N_META = 16
CHUNK = 128
PAD = CHUNK
FIRST_VALID = PAD - N_META
ROPE_THETA = 10000.0
EPS = 1e-6
LB_FLOOR = 1e-30
RET_HEADS = 4
RET_DK = 128
RET_DV = 256
HG_HEADS = 8
HG_DK = 128
HG_DV = 128
DA_HEADS = 8
DA_DH = 64
DA_DV = 2 * DA_DH
N_BRANCH = 3
D_FF = 2816
CONV_W = 3
MASK_VALUE = -1e30
SUB = 16

RET_QK_W = RET_HEADS * RET_DK
RET_V_W = RET_HEADS * RET_DV
HG_K_W = HG_HEADS * HG_DK
HG_V_W = HG_HEADS * HG_DV
DA_QK_W = DA_HEADS * 2 * DA_DH
DA_V_W = DA_HEADS * DA_DV
IN_SPLITS = (RET_QK_W, RET_QK_W, RET_V_W, RET_V_W, HG_K_W, HG_K_W, HG_V_W, HG_V_W,
             DA_QK_W, DA_QK_W, DA_V_W, N_BRANCH * D_MODEL)
IN_WIDTH = sum(IN_SPLITS)
(OFF_RQ, OFF_RK, OFF_RV, OFF_RG, OFF_HQ, OFF_HF, OFF_HI, OFF_HG,
 OFF_DQ, OFF_DK, OFF_DV, OFF_MG) = [sum(IN_SPLITS[:i]) for i in range(len(IN_SPLITS))]

F32 = jnp.float32
BF16 = jnp.bfloat16
LANES = 128
VMEM_LIMIT = 56 * 1024 * 1024


def _cparams(sem):
    return pltpu.CompilerParams(dimension_semantics=sem, vmem_limit_bytes=VMEM_LIMIT)


def _dot(a, b):
    return jnp.dot(a, b, preferred_element_type=F32)


def _dot_nt(a, b):
    return lax.dot_general(a, b, (((1,), (1,)), ((), ())), preferred_element_type=F32)


def _silu(x):
    return x * (1.0 / (1.0 + jnp.exp(-x)))


def _sigmoid(x):
    return 1.0 / (1.0 + jnp.exp(-x))


def _rms_rows(x):
    return x * lax.rsqrt(jnp.mean(x * x, axis=-1, keepdims=True) + EPS)


def _inproj_kernel(x_ref, g_ref, w_ref, o_ref, hn_ref):
    @pl.when(pl.program_id(1) == 0)
    def _():
        hn_ref[...] = (_rms_rows(x_ref[...]) * g_ref[...]).astype(BF16)

    o_ref[...] = _dot(hn_ref[...], w_ref[...]).astype(o_ref.dtype)


def _inproj(h2, g, w, tm=1280, tn=1024):
    t, d = h2.shape
    n = w.shape[1]
    return pl.pallas_call(
        _inproj_kernel,
        grid=(t // tm, n // tn),
        in_specs=[pl.BlockSpec((tm, d), lambda i, j: (i, 0)),
                  pl.BlockSpec((1, d), lambda i, j: (0, 0)),
                  pl.BlockSpec((d, tn), lambda i, j: (0, j))],
        out_specs=pl.BlockSpec((tm, tn), lambda i, j: (i, j)),
        out_shape=jax.ShapeDtypeStruct((t, n), F32),
        scratch_shapes=[pltpu.VMEM((tm, d), BF16)],
        compiler_params=_cparams(("parallel", "arbitrary")),
        name="inproj",
    )(h2, g, w)


def _ret_kernel(q_ref, k_ref, v_ref, g_ref, cos_ref, sin_ref, intra_ref, qd_ref, kd_ref, cd_ref,
                o_ref, st_ref, *, tl):
    lblk = pl.program_id(2)

    @pl.when(lblk == 0)
    def _():
        st_ref[...] = jnp.zeros_like(st_ref)

    intra = intra_ref[...]
    qd = qd_ref[...]
    kd = kd_ref[...]
    cd = cd_ref[...]
    row = lax.broadcasted_iota(jnp.int32, (CHUNK, RET_DK), 0)

    def body(c, carry):
        r = pl.multiple_of(c * CHUNK, CHUNK)
        rows = pl.ds(r, CHUNK)
        cos = cos_ref[rows, :]
        sin = sin_ref[rows, :]
        q = q_ref[rows, :]
        k = k_ref[rows, :]
        qr = q * cos + pltpu.roll(q, RET_DK // 2, 1) * sin
        kr = (k * cos + pltpu.roll(k, RET_DK // 2, 1) * sin) * RET_DK ** -0.5
        kr = jnp.where(lblk * tl + r + row >= FIRST_VALID, kr, 0.0)
        v = v_ref[rows, :].astype(BF16)
        state = st_ref[...]
        s = _dot_nt(qr.astype(BF16), kr.astype(BF16)) * intra
        o = _dot(s.astype(BF16), v) + _dot((qr * qd).astype(BF16), state.astype(BF16))
        st_ref[...] = cd * state + _dot((kr * kd).T.astype(BF16), v)
        o_ref[rows, :] = (_rms_rows(o) * _silu(g_ref[rows, :])).astype(o_ref.dtype)
        return carry

    lax.fori_loop(0, tl // CHUNK, body, 0)


def _retention(proj, cos, sin, intra, qd, kd, cd, tl=1664):
    b, l, _ = proj.shape

    def col(off, w):
        return lambda bi, hi, li: (bi, li, off // w + hi)

    return pl.pallas_call(
        functools.partial(_ret_kernel, tl=tl),
        grid=(b, RET_HEADS, l // tl),
        in_specs=[pl.BlockSpec((None, tl, RET_DK), col(OFF_RQ, RET_DK)),
                  pl.BlockSpec((None, tl, RET_DK), col(OFF_RK, RET_DK)),
                  pl.BlockSpec((None, tl, RET_DV), col(OFF_RV, RET_DV)),
                  pl.BlockSpec((None, tl, RET_DV), col(OFF_RG, RET_DV)),
                  pl.BlockSpec((tl, RET_DK), lambda bi, hi, li: (li, 0)),
                  pl.BlockSpec((tl, RET_DK), lambda bi, hi, li: (li, 0)),
                  pl.BlockSpec((None, CHUNK, CHUNK), lambda bi, hi, li: (hi, 0, 0)),
                  pl.BlockSpec((None, CHUNK, RET_DK), lambda bi, hi, li: (hi, 0, 0)),
                  pl.BlockSpec((None, CHUNK, RET_DK), lambda bi, hi, li: (hi, 0, 0)),
                  pl.BlockSpec((None, 1, RET_DV), lambda bi, hi, li: (hi, 0, 0))],
        out_specs=pl.BlockSpec((None, tl, RET_DV), lambda bi, hi, li: (bi, li, hi)),
        out_shape=jax.ShapeDtypeStruct((b, l, RET_V_W), BF16),
        scratch_shapes=[pltpu.VMEM((RET_DK, RET_DV), F32)],
        compiler_params=_cparams(("parallel", "parallel", "arbitrary")),
        name="retention",
    )(proj, proj, proj, proj, cos, sin, intra, qd, kd, cd)


def _hg_kernel(q_ref, f_ref, i_ref, g_ref, lb_ref, o_ref, st_ref, q_sc, k_sc, cb_sc, *, tl):
    lblk = pl.program_id(2)

    @pl.when(lblk == 0)
    def _():
        st_ref[...] = jnp.zeros_like(st_ref)

    lb = lb_ref[...]
    log_lb = jnp.log(jnp.maximum(lb, LB_FLOOR))
    log1m_lb = jnp.log1p(-lb)
    one_m_lb = 1.0 - lb
    row = lax.broadcasted_iota(jnp.int32, (CHUNK, HG_DK), 0)
    lane = lax.broadcasted_iota(jnp.int32, (CHUNK, HG_DK), 1)
    tri = (row >= lane).astype(F32)
    srow = lax.broadcasted_iota(jnp.int32, (SUB, CHUNK), 0)
    slane = lax.broadcasted_iota(jnp.int32, (SUB, CHUNK), 1)

    def body(c, carry):
        r = pl.multiple_of(c * CHUNK, CHUNK)
        rows = pl.ds(r, CHUNK)
        z = f_ref[rows, :]
        q = q_ref[rows, :]
        log_sig = jnp.minimum(z, 0.0) - jnp.log1p(jnp.exp(-jnp.abs(z)))
        bb = log1m_lb + log_sig
        log_f = jnp.maximum(log_lb, bb) + jnp.log1p(jnp.exp(-jnp.abs(log_lb - bb)))
        k = one_m_lb * (1.0 / (1.0 + jnp.exp(z)))
        v = jnp.where(lblk * tl + r + row >= FIRST_VALID, i_ref[rows, :], 0.0).astype(BF16)
        cb = jnp.dot(tri, log_f, precision=lax.Precision.HIGHEST, preferred_element_type=F32)
        q_sc[...] = q
        k_sc[...] = k
        cb_sc[...] = cb
        state = st_ref[...]
        cend = cb_sc[CHUNK - 1:CHUNK, :]
        o = _dot((q * jnp.exp(cb)).astype(BF16), state.astype(BF16))
        kdec = k * jnp.exp(cend - cb)
        dec_col = jnp.broadcast_to(jnp.exp(cend), (HG_DK, HG_DK)).T
        st_ref[...] = dec_col * state + _dot(kdec.T.astype(BF16), v)

        blocks = []
        for i in range(CHUNK // SUB):
            r0 = i * SUB
            qs = q_sc[r0:r0 + SUB, :]
            cbs = cb_sc[r0:r0 + SUB, :]
            if i > 0:
                ref_pt = cb_sc[r0 - 1:r0, :]
                qt = qs * jnp.exp(cbs - ref_pt)
                kt = k * jnp.exp(jnp.minimum(ref_pt - cb, 0.0))
                a = jnp.where(slane < r0, _dot_nt(qt.astype(BF16), kt.astype(BF16)), 0.0)
            else:
                a = jnp.zeros((SUB, CHUNK), F32)
            for j in range(SUB):
                kj = k_sc[r0 + j:r0 + j + 1, :]
                cbj = cb_sc[r0 + j:r0 + j + 1, :]
                causal = srow >= j
                t = qs * kj * jnp.exp(jnp.where(causal, cbs - cbj, 0.0))
                colv = jnp.sum(t, axis=1, keepdims=True)
                a = jnp.where((slane == r0 + j) & causal, colv, a)
            blocks.append(a)
        amat = jnp.concatenate(blocks, axis=0)
        o = o + _dot(amat.astype(BF16), v)
        o_ref[rows, :] = (_rms_rows(o) * _silu(g_ref[rows, :])).astype(o_ref.dtype)
        return carry

    lax.fori_loop(0, tl // CHUNK, body, 0)


def _hgrn2(proj, lb, tl=1664):
    b, l, _ = proj.shape

    def col(off):
        return lambda bi, hi, li: (bi, li, off // HG_DK + hi)

    return pl.pallas_call(
        functools.partial(_hg_kernel, tl=tl),
        grid=(b, HG_HEADS, l // tl),
        in_specs=[pl.BlockSpec((None, tl, HG_DK), col(OFF_HQ)),
                  pl.BlockSpec((None, tl, HG_DK), col(OFF_HF)),
                  pl.BlockSpec((None, tl, HG_DV), col(OFF_HI)),
                  pl.BlockSpec((None, tl, HG_DV), col(OFF_HG)),
                  pl.BlockSpec((None, 1, HG_DK), lambda bi, hi, li: (hi, 0, 0))],
        out_specs=pl.BlockSpec((None, tl, HG_DV), lambda bi, hi, li: (bi, li, hi)),
        out_shape=jax.ShapeDtypeStruct((b, l, HG_V_W), BF16),
        scratch_shapes=[pltpu.VMEM((HG_DK, HG_DV), F32),
                        pltpu.VMEM((CHUNK, HG_DK), F32),
                        pltpu.VMEM((CHUNK, HG_DK), F32),
                        pltpu.VMEM((CHUNK, HG_DK), F32)],
        compiler_params=_cparams(("parallel", "parallel", "arbitrary")),
        name="hgrn2",
    )(proj, proj, proj, proj, lb)


def _da_prep_kernel(q_ref, k_ref, v_ref, cos_ref, sin_ref, qo_ref, kt_ref, vo_ref):
    cos = cos_ref[...]
    sin = sin_ref[...]
    lane = lax.broadcasted_iota(jnp.int32, cos.shape, 1)
    first_half = (lane % DA_DH) < DA_DH // 2

    def rope(x):
        swapped = jnp.where(first_half, pltpu.roll(x, LANES - DA_DH // 2, 1), pltpu.roll(x, DA_DH // 2, 1))
        return x * cos + swapped * sin

    qo_ref[...] = (rope(q_ref[...]) * DA_DH ** -0.5).astype(BF16)
    kt_ref[...] = rope(k_ref[...]).T.astype(BF16)
    vo_ref[...] = v_ref[...].astype(BF16)


def _da_prep(proj, cos, sin, tl=640):
    b, l, _ = proj.shape

    def col(off):
        return lambda bi, hi, li: (bi, li, off // LANES + hi)

    return pl.pallas_call(
        _da_prep_kernel,
        grid=(b, DA_HEADS, l // tl),
        in_specs=[pl.BlockSpec((None, tl, LANES), col(OFF_DQ)),
                  pl.BlockSpec((None, tl, LANES), col(OFF_DK)),
                  pl.BlockSpec((None, tl, LANES), col(OFF_DV)),
                  pl.BlockSpec((tl, LANES), lambda bi, hi, li: (li, 0)),
                  pl.BlockSpec((tl, LANES), lambda bi, hi, li: (li, 0))],
        out_specs=[pl.BlockSpec((None, tl, LANES), lambda bi, hi, li: (bi, li, hi)),
                   pl.BlockSpec((None, LANES, tl), lambda bi, hi, li: (bi, hi, li)),
                   pl.BlockSpec((None, tl, LANES), lambda bi, hi, li: (bi, li, hi))],
        out_shape=[jax.ShapeDtypeStruct((b, l, DA_QK_W), BF16),
                   jax.ShapeDtypeStruct((b, DA_QK_W, l), BF16),
                   jax.ShapeDtypeStruct((b, l, DA_V_W), BF16)],
        compiler_params=_cparams(("parallel", "parallel", "parallel")),
        name="da_prep",
    )(proj, proj, proj, cos, sin)


def _da_kernel(q_ref, kt_ref, v_ref, lam_ref, sg_ref, o_ref, m_ref, l_ref, acc_ref, *, tq, lambda_init):
    qi = pl.program_id(2)
    q = q_ref[...]
    lane = lax.broadcasted_iota(jnp.int32, q.shape, 1)
    zero = jnp.zeros_like(q)
    qmaps = (jnp.where(lane < DA_DH, q, zero), jnp.where(lane >= DA_DH, q, zero))
    m_ref[...] = jnp.full_like(m_ref, MASK_VALUE)
    l_ref[...] = jnp.zeros_like(l_ref)
    acc_ref[...] = jnp.zeros_like(acc_ref)
    qpos = qi * tq + lax.broadcasted_iota(jnp.int32, (tq, tq), 0)
    kloc = lax.broadcasted_iota(jnp.int32, (tq, tq), 1)

    def step(kb, allowed):
        cols = pl.ds(pl.multiple_of(kb * tq, tq), tq)
        kt = kt_ref[:, cols]
        v = v_ref[cols, :]
        for m in range(2):
            s = _dot(qmaps[m], kt)
            if allowed is not None:
                s = jnp.where(allowed, s, MASK_VALUE)
            m_prev = m_ref[m]
            m_new = jnp.maximum(m_prev, jnp.max(s, axis=1, keepdims=True))
            p = jnp.exp(s - m_new)
            alpha = jnp.exp(m_prev - m_new)
            l_ref[m] = alpha * l_ref[m] + jnp.sum(p, axis=1, keepdims=True)
            acc_ref[m] = alpha * acc_ref[m] + _dot(p.astype(BF16), v)
            m_ref[m] = m_new

    step(0, (kloc <= qpos) & (kloc >= FIRST_VALID))

    def body(kb, carry):
        step(kb, None)
        return carry

    lax.fori_loop(1, qi, body, 0)

    @pl.when(qi > 0)
    def _():
        step(qi, kloc + qi * tq <= qpos)

    lp = lam_ref[...]
    lam = (jnp.exp(jnp.sum(lp[0:1] * lp[1:2], axis=1, keepdims=True))
           - jnp.exp(jnp.sum(lp[2:3] * lp[3:4], axis=1, keepdims=True)) + lambda_init)
    o = acc_ref[0] / l_ref[0] - lam * (acc_ref[1] / l_ref[1])
    o_ref[...] = (_rms_rows(o) * sg_ref[...] * (1.0 - lambda_init)).astype(o_ref.dtype)


def _diff_attn(qr, kt, vb, lam_p, subln_g, lambda_init, tq=640):
    b, l, _ = qr.shape
    return pl.pallas_call(
        functools.partial(_da_kernel, tq=tq, lambda_init=lambda_init),
        grid=(b, DA_HEADS, l // tq),
        in_specs=[pl.BlockSpec((None, tq, LANES), lambda bi, hi, qi: (bi, qi, hi)),
                  pl.BlockSpec((None, LANES, l), lambda bi, hi, qi: (bi, hi, 0)),
                  pl.BlockSpec((None, l, DA_DV), lambda bi, hi, qi: (bi, 0, hi)),
                  pl.BlockSpec((4, DA_DH), lambda bi, hi, qi: (0, 0)),
                  pl.BlockSpec((1, DA_DV), lambda bi, hi, qi: (0, 0))],
        out_specs=pl.BlockSpec((None, tq, DA_DV), lambda bi, hi, qi: (bi, qi, hi)),
        out_shape=jax.ShapeDtypeStruct((b, l, DA_V_W), BF16),
        scratch_shapes=[pltpu.VMEM((2, tq, 1), F32),
                        pltpu.VMEM((2, tq, 1), F32),
                        pltpu.VMEM((2, tq, DA_DV), F32)],
        compiler_params=_cparams(("parallel", "parallel", "arbitrary")),
        name="diff_attn",
    )(qr, kt, vb, lam_p, subln_g)


def _merge_kernel(o0_ref, o1_ref, o2_ref, g0_ref, g1_ref, g2_ref, wb_ref, wo_ref, h_ref, out_ref):
    y = None
    for n, (o_ref, g_ref) in enumerate(((o0_ref, g0_ref), (o1_ref, g1_ref), (o2_ref, g2_ref))):
        term = _sigmoid(g_ref[...]) * _dot(o_ref[...], wb_ref[n])
        y = term if y is None else y + term
    out_ref[...] = h_ref[...] + _dot(y.astype(BF16), wo_ref[...])


def _merge(o_ret, o_hg, o_da, proj2, wb, wo, h2, tm=256):
    t, d = h2.shape
    row = lambda i: (i, 0)
    const2 = lambda i: (0, 0)
    return pl.pallas_call(
        _merge_kernel,
        grid=(t // tm,),
        in_specs=[pl.BlockSpec((tm, d), row), pl.BlockSpec((tm, d), row), pl.BlockSpec((tm, d), row),
                  pl.BlockSpec((tm, d), lambda i: (i, OFF_MG // D_MODEL)),
                  pl.BlockSpec((tm, d), lambda i: (i, OFF_MG // D_MODEL + 1)),
                  pl.BlockSpec((tm, d), lambda i: (i, OFF_MG // D_MODEL + 2)),
                  pl.BlockSpec((N_BRANCH, d, d), lambda i: (0, 0, 0)),
                  pl.BlockSpec((d, d), const2),
                  pl.BlockSpec((tm, d), row)],
        out_specs=pl.BlockSpec((tm, d), row),
        out_shape=jax.ShapeDtypeStruct((t, d), F32),
        compiler_params=_cparams(("parallel",)),
        name="merge",
    )(o_ret, o_hg, o_da, proj2, proj2, proj2, wb, wo, h2)


FFN_HALO = 8


def _ffn_kernel(xp_ref, x_ref, g_ref, wg_ref, wv_ref, cwg_ref, cwv_ref, cbg_ref, cbv_ref, wo_ref,
                out_ref, hn_ref, acc_ref, *, tm, seq):
    i = pl.program_id(0)
    j = pl.program_id(1)

    @pl.when(j == 0)
    def _():
        x = jnp.concatenate([xp_ref[...], x_ref[...]], axis=0)
        hn_ref[...] = (_rms_rows(x) * g_ref[...]).astype(BF16)
        acc_ref[...] = x_ref[...]

    hn = hn_ref[...]
    rows = i * tm - FFN_HALO + lax.broadcasted_iota(jnp.int32, (tm + FFN_HALO, 1), 0)
    valid = lax.rem(rows + seq, seq) >= FIRST_VALID

    def conv(w_ref, cw_ref, cb_ref):
        u = jnp.where(valid, _dot(hn, w_ref[...]), 0.0)
        cw = cw_ref[...]
        return (cb_ref[...] + cw[0:1] * pltpu.roll(u, 2, 0)[FFN_HALO:]
                + cw[1:2] * pltpu.roll(u, 1, 0)[FFN_HALO:] + cw[2:3] * u[FFN_HALO:])

    act = _silu(conv(wg_ref, cwg_ref, cbg_ref)) * conv(wv_ref, cwv_ref, cbv_ref)
    acc_ref[...] += _dot(act.astype(BF16), wo_ref[...])

    @pl.when(j == pl.num_programs(1) - 1)
    def _():
        out_ref[...] = acc_ref[...]


def _ffn(h2, g, w_in, conv_w, conv_b, w_out, seq, tm=1280, fk=256):
    t, d = h2.shape
    nf = D_FF // fk
    hb = tm // FFN_HALO
    return pl.pallas_call(
        functools.partial(_ffn_kernel, tm=tm, seq=seq),
        grid=(t // tm, nf),
        in_specs=[pl.BlockSpec((FFN_HALO, d), lambda i, j: (jnp.maximum(i * hb - 1, 0), 0)),
                  pl.BlockSpec((tm, d), lambda i, j: (i, 0)),
                  pl.BlockSpec((1, d), lambda i, j: (0, 0)),
                  pl.BlockSpec((d, fk), lambda i, j: (0, j)),
                  pl.BlockSpec((d, fk), lambda i, j: (0, nf + j)),
                  pl.BlockSpec((CONV_W, fk), lambda i, j: (0, j)),
                  pl.BlockSpec((CONV_W, fk), lambda i, j: (0, nf + j)),
                  pl.BlockSpec((1, fk), lambda i, j: (0, j)),
                  pl.BlockSpec((1, fk), lambda i, j: (0, nf + j)),
                  pl.BlockSpec((fk, d), lambda i, j: (j, 0))],
        out_specs=pl.BlockSpec((tm, d), lambda i, j: (i, 0)),
        out_shape=jax.ShapeDtypeStruct((t, d), F32),
        scratch_shapes=[pltpu.VMEM((tm + FFN_HALO, d), BF16), pltpu.VMEM((tm, d), F32)],
        compiler_params=_cparams(("parallel", "arbitrary")),
        name="ffn",
    )(h2, h2, g, w_in, w_in, conv_w, conv_w, conv_b, conv_b, w_out)


def _final_kernel(x_ref, g_ref, o_ref):
    o_ref[...] = _rms_rows(x_ref[...]) * g_ref[...]


def _final_norm(h, g):
    b, l, d = h.shape
    s = l - PAD
    return pl.pallas_call(
        _final_kernel,
        grid=(b, s // PAD),
        in_specs=[pl.BlockSpec((None, PAD, d), lambda bi, i: (bi, i + 1, 0)),
                  pl.BlockSpec((1, d), lambda bi, i: (0, 0))],
        out_specs=pl.BlockSpec((None, PAD, d), lambda bi, i: (bi, i, 0)),
        out_shape=jax.ShapeDtypeStruct((b, s, d), F32),
        compiler_params=_cparams(("parallel", "parallel")),
        name="final_norm",
    )(h, g)


def _rope_tables(pos, d, reps):
    inv = ROPE_THETA ** (-jnp.arange(0, d, 2, dtype=F32) / d)
    ang = pos.astype(F32)[:, None] * inv[None, :]
    cos = jnp.cos(ang)
    sin = jnp.sin(ang)
    return jnp.tile(jnp.concatenate([cos, cos], axis=1), (1, reps)), jnp.tile(jnp.concatenate([-sin, sin], axis=1), (1, reps))


def _retention_tables():
    log_g = jnp.log1p(-jnp.exp2(-5.0 - jnp.arange(RET_HEADS, dtype=F32)))
    idx = jnp.arange(CHUNK, dtype=F32)
    gap = idx[:, None] - idx[None, :]
    intra = jnp.where(gap >= 0, jnp.exp(log_g[:, None, None] * jnp.maximum(gap, 0.0)), 0.0)
    q_dec = jnp.exp(log_g[:, None] * (idx[None, :] + 1.0))
    k_dec = jnp.exp(log_g[:, None] * (CHUNK - 1.0 - idx[None, :]))
    c_dec = jnp.exp(log_g * CHUNK)
    qd = jnp.broadcast_to(q_dec[:, :, None], (RET_HEADS, CHUNK, RET_DK))
    kd = jnp.broadcast_to(k_dec[:, :, None], (RET_HEADS, CHUNK, RET_DK))
    cd = jnp.broadcast_to(c_dec[:, None, None], (RET_HEADS, 1, RET_DV))
    return intra, qd, kd, cd


def kernel(x, meta, norm_mix_g, w_in, w_branch, w_out, hg_lb, da_lambda, da_subln_g, norm_ffn_g,
           w_ffn_in, ffn_conv_w, ffn_conv_b, w_ffn_out, norm_final_g):
    b, s, d = x.shape
    l = PAD + s
    h = jnp.concatenate([jnp.zeros((b, PAD - N_META, d), x.dtype),
                         jnp.broadcast_to(meta[None].astype(x.dtype), (b, N_META, d)), x], axis=1)
    pos = jnp.arange(l) - FIRST_VALID
    lb_soft = jax.nn.softmax(hg_lb.astype(F32), axis=0)
    lbs = jnp.cumsum(lb_soft, axis=0) - lb_soft[0]
    ret_cos, ret_sin = _rope_tables(pos, RET_DK, 1)
    da_cos, da_sin = _rope_tables(pos, DA_DH, 2)
    intra, qd, kd, cd = _retention_tables()

    h2 = h.reshape(b * l, d)
    for li in range(DEPTH):
        lambda_init = 0.8 - 0.6 * math.exp(-0.3 * li)
        proj2 = _inproj(h2, norm_mix_g[li][None], w_in[li].astype(BF16))
        proj = proj2.reshape(b, l, IN_WIDTH)
        o_ret = _retention(proj, ret_cos, ret_sin, intra, qd, kd, cd)
        o_hg = _hgrn2(proj, lbs[li].reshape(HG_HEADS, 1, HG_DK))
        qr, kt, vb = _da_prep(proj, da_cos, da_sin)
        o_da = _diff_attn(qr, kt, vb, da_lambda[li].astype(F32), da_subln_g[li].astype(F32)[None], lambda_init)
        h2 = _merge(o_ret.reshape(b * l, -1), o_hg.reshape(b * l, -1), o_da.reshape(b * l, -1), proj2,
                    w_branch[li].astype(BF16), w_out[li].astype(BF16), h2)
        h2 = _ffn(h2, norm_ffn_g[li][None], w_ffn_in[li].astype(BF16), ffn_conv_w[li], ffn_conv_b[li][None],
                  w_ffn_out[li].astype(BF16), l)
    return _final_norm(h2.reshape(b, l, d), norm_final_g[None])
```

```python
import functools
import math

import jax
import jax.numpy as jnp
from jax import lax
from jax.experimental import pallas as pl
from jax.experimental.pallas import tpu as pltpu

D_MODEL = 1024
DEPTH = 2
N_META = 16
CHUNK = 128
PAD = CHUNK
FIRST_VALID = PAD - N_META
ROPE_THETA = 10000.0
EPS = 1e-6
LB_FLOOR = 1e-30
RET_HEADS = 4
RET_DK = 128
RET_DV = 256
HG_HEADS = 8
HG_DK = 128
HG_DV = 128
DA_HEADS = 8
DA_DH = 64
DA_DV = 2 * DA_DH
N_BRANCH = 3
D_FF = 2816
CONV_W = 3
MASK_VALUE = -1e30
LOG2E = 1.4426950408889634
SUB = 16

RET_QK_W = RET_HEADS * RET_DK
RET_V_W = RET_HEADS * RET_DV
HG_K_W = HG_HEADS * HG_DK
HG_V_W = HG_HEADS * HG_DV
DA_QK_W = DA_HEADS * 2 * DA_DH
DA_V_W = DA_HEADS * DA_DV
IN_SPLITS = (RET_QK_W, RET_QK_W, RET_V_W, RET_V_W, HG_K_W, HG_K_W, HG_V_W, HG_V_W,
             DA_QK_W, DA_QK_W, DA_V_W, N_BRANCH * D_MODEL)
IN_WIDTH = sum(IN_SPLITS)
(OFF_RQ, OFF_RK, OFF_RV, OFF_RG, OFF_HQ, OFF_HF, OFF_HI, OFF_HG,
 OFF_DQ, OFF_DK, OFF_DV, OFF_MG) = [sum(IN_SPLITS[:i]) for i in range(len(IN_SPLITS))]

F32 = jnp.float32
BF16 = jnp.bfloat16
LANES = 128
VMEM_LIMIT = 56 * 1024 * 1024


def _cparams(sem):
    return pltpu.CompilerParams(dimension_semantics=sem, vmem_limit_bytes=VMEM_LIMIT)


def _dot(a, b):
    return jnp.dot(a, b, preferred_element_type=F32)


def _dot_nt(a, b):
    return lax.dot_general(a, b, (((1,), (1,)), ((), ())), preferred_element_type=F32)


def _silu(x):
    return x * (1.0 / (1.0 + jnp.exp(-x)))


def _sigmoid(x):
    return 1.0 / (1.0 + jnp.exp(-x))


def _rms_rows(x):
    return x * lax.rsqrt(jnp.mean(x * x, axis=-1, keepdims=True) + EPS)


def _inproj_kernel(x_ref, g_ref, w_ref, o_ref, hn_ref):
    @pl.when(pl.program_id(1) == 0)
    def _():
        hn_ref[...] = (_rms_rows(x_ref[...]) * g_ref[...]).astype(BF16)

    o_ref[...] = _dot(hn_ref[...], w_ref[...]).astype(o_ref.dtype)


def _inproj(h2, g, w, tm=1280, tn=1024):
    t, d = h2.shape
    n = w.shape[1]
    return pl.pallas_call(
        _inproj_kernel,
        grid=(t // tm, n // tn),
        in_specs=[pl.BlockSpec((tm, d), lambda i, j: (i, 0)),
                  pl.BlockSpec((1, d), lambda i, j: (0, 0)),
                  pl.BlockSpec((d, tn), lambda i, j: (0, j))],
        out_specs=pl.BlockSpec((tm, tn), lambda i, j: (i, j)),
        out_shape=jax.ShapeDtypeStruct((t, n), F32),
        scratch_shapes=[pltpu.VMEM((tm, d), BF16)],
        compiler_params=_cparams(("parallel", "arbitrary")),
        name="inproj",
    )(h2, g, w)


def _ret_kernel(q_ref, k_ref, v_ref, g_ref, cos_ref, sin_ref, intra_ref, qd_ref, kd_ref, cd_ref,
                o_ref, st_ref, *, tl):
    lblk = pl.program_id(2)

    @pl.when(lblk == 0)
    def _():
        st_ref[...] = jnp.zeros_like(st_ref)

    intra = intra_ref[...]
    qd = qd_ref[...]
    kd = kd_ref[...]
    cd = cd_ref[...]
    row = lax.broadcasted_iota(jnp.int32, (CHUNK, RET_DK), 0)

    def body(c, carry):
        r = pl.multiple_of(c * CHUNK, CHUNK)
        rows = pl.ds(r, CHUNK)
        cos = cos_ref[rows, :]
        sin = sin_ref[rows, :]
        q = q_ref[rows, :]
        k = k_ref[rows, :]
        qr = q * cos + pltpu.roll(q, RET_DK // 2, 1) * sin
        kr = (k * cos + pltpu.roll(k, RET_DK // 2, 1) * sin) * RET_DK ** -0.5
        kr = jnp.where(lblk * tl + r + row >= FIRST_VALID, kr, 0.0)
        v = v_ref[rows, :].astype(BF16)
        state = st_ref[...]
        s = _dot_nt(qr.astype(BF16), kr.astype(BF16)) * intra
        o = _dot(s.astype(BF16), v) + _dot((qr * qd).astype(BF16), state.astype(BF16))
        st_ref[...] = cd * state + _dot((kr * kd).T.astype(BF16), v)
        o_ref[rows, :] = (_rms_rows(o) * _silu(g_ref[rows, :])).astype(o_ref.dtype)
        return carry

    lax.fori_loop(0, tl // CHUNK, body, 0)


def _retention(proj, cos, sin, intra, qd, kd, cd, tl=1664):
    b, l, _ = proj.shape

    def col(off, w):
        return lambda bi, hi, li: (bi, li, off // w + hi)

    return pl.pallas_call(
        functools.partial(_ret_kernel, tl=tl),
        grid=(b, RET_HEADS, l // tl),
        in_specs=[pl.BlockSpec((None, tl, RET_DK), col(OFF_RQ, RET_DK)),
                  pl.BlockSpec((None, tl, RET_DK), col(OFF_RK, RET_DK)),
                  pl.BlockSpec((None, tl, RET_DV), col(OFF_RV, RET_DV)),
                  pl.BlockSpec((None, tl, RET_DV), col(OFF_RG, RET_DV)),
                  pl.BlockSpec((tl, RET_DK), lambda bi, hi, li: (li, 0)),
                  pl.BlockSpec((tl, RET_DK), lambda bi, hi, li: (li, 0)),
                  pl.BlockSpec((None, CHUNK, CHUNK), lambda bi, hi, li: (hi, 0, 0)),
                  pl.BlockSpec((None, CHUNK, RET_DK), lambda bi, hi, li: (hi, 0, 0)),
                  pl.BlockSpec((None, CHUNK, RET_DK), lambda bi, hi, li: (hi, 0, 0)),
                  pl.BlockSpec((None, 1, RET_DV), lambda bi, hi, li: (hi, 0, 0))],
        out_specs=pl.BlockSpec((None, tl, RET_DV), lambda bi, hi, li: (bi, li, hi)),
        out_shape=jax.ShapeDtypeStruct((b, l, RET_V_W), BF16),
        scratch_shapes=[pltpu.VMEM((RET_DK, RET_DV), F32)],
        compiler_params=_cparams(("parallel", "parallel", "arbitrary")),
        name="retention",
    )(proj, proj, proj, proj, cos, sin, intra, qd, kd, cd)


def _hg_kernel(q_ref, f_ref, i_ref, g_ref, lb_ref, o_ref, st_ref, q_sc, k_sc, cb_sc, *, tl):
    lblk = pl.program_id(2)

    @pl.when(lblk == 0)
    def _():
        st_ref[...] = jnp.zeros_like(st_ref)

    lb = lb_ref[...]
    log_lb = jnp.log(jnp.maximum(lb, LB_FLOOR))
    log1m_lb = jnp.log1p(-lb)
    one_m_lb = 1.0 - lb
    row = lax.broadcasted_iota(jnp.int32, (CHUNK, HG_DK), 0)
    lane = lax.broadcasted_iota(jnp.int32, (CHUNK, HG_DK), 1)
    tri = (row >= lane).astype(F32)
    srow = lax.broadcasted_iota(jnp.int32, (SUB, CHUNK), 0)
    slane = lax.broadcasted_iota(jnp.int32, (SUB, CHUNK), 1)

    def body(c, carry):
        r = pl.multiple_of(c * CHUNK, CHUNK)
        rows = pl.ds(r, CHUNK)
        z = f_ref[rows, :]
        q = q_ref[rows, :]
        log_sig = jnp.minimum(z, 0.0) - jnp.log1p(jnp.exp(-jnp.abs(z)))
        bb = log1m_lb + log_sig
        log_f = jnp.maximum(log_lb, bb) + jnp.log1p(jnp.exp(-jnp.abs(log_lb - bb)))
        k = one_m_lb * (1.0 / (1.0 + jnp.exp(z)))
        v = jnp.where(lblk * tl + r + row >= FIRST_VALID, i_ref[rows, :], 0.0).astype(BF16)
        cb = jnp.dot(tri, log_f, precision=lax.Precision.HIGHEST, preferred_element_type=F32)
        q_sc[...] = q
        k_sc[...] = k
        cb_sc[...] = cb
        state = st_ref[...]
        cend = cb_sc[CHUNK - 1:CHUNK, :]
        o = _dot((q * jnp.exp(cb)).astype(BF16), state.astype(BF16))
        kdec = k * jnp.exp(cend - cb)
        dec_col = jnp.broadcast_to(jnp.exp(cend), (HG_DK, HG_DK)).T
        st_ref[...] = dec_col * state + _dot(kdec.T.astype(BF16), v)

        blocks = []
        for i in range(CHUNK // SUB):
            r0 = i * SUB
            qs = q_sc[r0:r0 + SUB, :]
            cbs = cb_sc[r0:r0 + SUB, :]
            if i > 0:
                ref_pt = cb_sc[r0 - 1:r0, :]
                qt = qs * jnp.exp(cbs - ref_pt)
                kt = k * jnp.exp(jnp.minimum(ref_pt - cb, 0.0))
                a = jnp.where(slane < r0, _dot_nt(qt.astype(BF16), kt.astype(BF16)), 0.0)
            else:
                a = jnp.zeros((SUB, CHUNK), F32)
            for j in range(SUB):
                kj = k_sc[r0 + j:r0 + j + 1, :]
                cbj = cb_sc[r0 + j:r0 + j + 1, :]
                causal = srow >= j
                t = qs * kj * jnp.exp(jnp.where(causal, cbs - cbj, 0.0))
                colv = jnp.sum(t, axis=1, keepdims=True)
                a = jnp.where((slane == r0 + j) & causal, colv, a)
            blocks.append(a)
        amat = jnp.concatenate(blocks, axis=0)
        o = o + _dot(amat.astype(BF16), v)
        o_ref[rows, :] = (_rms_rows(o) * _silu(g_ref[rows, :])).astype(o_ref.dtype)
        return carry

    lax.fori_loop(0, tl // CHUNK, body, 0)


def _hgrn2(proj, lb, tl=1664):
    b, l, _ = proj.shape

    def col(off):
        return lambda bi, hi, li: (bi, li, off // HG_DK + hi)

    return pl.pallas_call(
        functools.partial(_hg_kernel, tl=tl),
        grid=(b, HG_HEADS, l // tl),
        in_specs=[pl.BlockSpec((None, tl, HG_DK), col(OFF_HQ)),
                  pl.BlockSpec((None, tl, HG_DK), col(OFF_HF)),
                  pl.BlockSpec((None, tl, HG_DV), col(OFF_HI)),
                  pl.BlockSpec((None, tl, HG_DV), col(OFF_HG)),
                  pl.BlockSpec((None, 1, HG_DK), lambda bi, hi, li: (hi, 0, 0))],
        out_specs=pl.BlockSpec((None, tl, HG_DV), lambda bi, hi, li: (bi, li, hi)),
        out_shape=jax.ShapeDtypeStruct((b, l, HG_V_W), BF16),
        scratch_shapes=[pltpu.VMEM((HG_DK, HG_DV), F32),
                        pltpu.VMEM((CHUNK, HG_DK), F32),
                        pltpu.VMEM((CHUNK, HG_DK), F32),
                        pltpu.VMEM((CHUNK, HG_DK), F32)],
        compiler_params=_cparams(("parallel", "parallel", "arbitrary")),
        name="hgrn2",
    )(proj, proj, proj, proj, lb)


def _da_prep_kernel(q_ref, k_ref, v_ref, cos_ref, sin_ref, qo_ref, kt_ref, vo_ref):
    cos = cos_ref[...]
    sin = sin_ref[...]
    lane = lax.broadcasted_iota(jnp.int32, cos.shape, 1)
    first_half = (lane % DA_DH) < DA_DH // 2

    def rope(x):
        swapped = jnp.where(first_half, pltpu.roll(x, LANES - DA_DH // 2, 1), pltpu.roll(x, DA_DH // 2, 1))
        return x * cos + swapped * sin

    qo_ref[...] = (rope(q_ref[...]) * (DA_DH ** -0.5 * LOG2E)).T.astype(BF16)
    kt_ref[...] = rope(k_ref[...]).astype(BF16)
    vo_ref[...] = v_ref[...].T.astype(BF16)


def _da_prep(proj, cos, sin, tl=640):
    b, l, _ = proj.shape

    def col(off):
        return lambda bi, hi, li: (bi, li, off // LANES + hi)

    return pl.pallas_call(
        _da_prep_kernel,
        grid=(b, DA_HEADS, l // tl),
        in_specs=[pl.BlockSpec((None, tl, LANES), col(OFF_DQ)),
                  pl.BlockSpec((None, tl, LANES), col(OFF_DK)),
                  pl.BlockSpec((None, tl, LANES), col(OFF_DV)),
                  pl.BlockSpec((tl, LANES), lambda bi, hi, li: (li, 0)),
                  pl.BlockSpec((tl, LANES), lambda bi, hi, li: (li, 0))],
        out_specs=[pl.BlockSpec((None, LANES, tl), lambda bi, hi, li: (bi, hi, li)),
                   pl.BlockSpec((None, tl, LANES), lambda bi, hi, li: (bi, li, hi)),
                   pl.BlockSpec((None, LANES, tl), lambda bi, hi, li: (bi, hi, li))],
        out_shape=[jax.ShapeDtypeStruct((b, DA_QK_W, l), BF16),
                   jax.ShapeDtypeStruct((b, l, DA_QK_W), BF16),
                   jax.ShapeDtypeStruct((b, DA_V_W, l), BF16)],
        compiler_params=_cparams(("parallel", "parallel", "parallel")),
        name="da_prep",
    )(proj, proj, proj, cos, sin)


def _da_kernel(qt_ref, k_ref, vt_ref, lam_ref, sg_ref, o_ref, m_ref, l_ref, acc_ref, sa_ref, ba_ref, sb_ref, bb_ref,
               *, tq, lambda_init):
    qi = pl.program_id(2)
    qt = qt_ref[...]
    row = lax.broadcasted_iota(jnp.int32, qt.shape, 0)
    zero = jnp.zeros_like(qt)
    qmaps = (jnp.where(row < DA_DH, qt, zero), jnp.where(row >= DA_DH, qt, zero))
    m_ref[...] = jnp.full_like(m_ref, MASK_VALUE)
    l_ref[...] = jnp.zeros_like(l_ref)
    acc_ref[...] = jnp.zeros_like(acc_ref)
    kloc = lax.broadcasted_iota(jnp.int32, (tq, tq), 0)
    qpos = qi * tq + lax.broadcasted_iota(jnp.int32, (tq, tq), 1)

    bufs = ((sa_ref, ba_ref), (sb_ref, bb_ref))

    def block_rows(kb):
        return pl.ds(pl.multiple_of(kb * tq, tq), tq)

    def scores(kb, allowed, buf):
        s_ref, bmax_ref = buf
        k = k_ref[block_rows(kb), :]
        for m in range(2):
            s = _dot(k, qmaps[m])
            if allowed is not None:
                s = jnp.where(allowed, s, MASK_VALUE)
            s_ref[m] = s
            bmax_ref[m] = jnp.max(s, axis=0, keepdims=True)

    def accumulate(kb, buf):
        s_ref, bmax_ref = buf
        vt = vt_ref[:, block_rows(kb)]
        for m in range(2):
            m_prev = m_ref[m]
            m_new = jnp.maximum(m_prev, bmax_ref[m])
            p = jnp.exp2(s_ref[m] - m_new)
            alpha = jnp.exp2(m_prev - m_new)
            l_ref[m] = alpha * l_ref[m] + jnp.sum(p, axis=0, keepdims=True)
            acc_ref[m] = alpha * acc_ref[m] + _dot(vt, p.astype(BF16))
            m_ref[m] = m_new

    scores(0, (kloc <= qpos) & (kloc >= FIRST_VALID), bufs[0])

    n_mid = jnp.maximum(qi - 1, 0)

    def pair(t, carry):
        j = 2 * t
        scores(j + 1, None, bufs[1])
        accumulate(j, bufs[0])
        scores(j + 2, None, bufs[0])
        accumulate(j + 1, bufs[1])
        return carry

    lax.fori_loop(0, n_mid // 2, pair, 0)
    odd = lax.rem(n_mid, 2) == 1

    @pl.when(odd)
    def _():
        scores(n_mid, None, bufs[1])
        accumulate(n_mid - 1, bufs[0])

    causal = kloc + qi * tq <= qpos
    for parity in (0, 1):
        @pl.when((qi > 0) & (odd == (parity == 1)))
        def _():
            scores(qi, causal, bufs[1 - parity])
            accumulate(qi - 1, bufs[parity])
            accumulate(qi, bufs[1 - parity])

    @pl.when(qi == 0)
    def _():
        accumulate(0, bufs[0])

    lp = lam_ref[...]
    lam = (jnp.exp(jnp.sum(lp[0:1] * lp[1:2], axis=1, keepdims=True))
           - jnp.exp(jnp.sum(lp[2:3] * lp[3:4], axis=1, keepdims=True)) + lambda_init)
    ot = acc_ref[0] / l_ref[0] - lam * (acc_ref[1] / l_ref[1])
    o_ref[...] = (_rms_rows(ot.T) * sg_ref[...] * (1.0 - lambda_init)).astype(o_ref.dtype)


def _diff_attn(qt, kr, vt, lam_p, subln_g, lambda_init, tq=640):
    b, l, _ = kr.shape
    return pl.pallas_call(
        functools.partial(_da_kernel, tq=tq, lambda_init=lambda_init),
        grid=(b, DA_HEADS, l // tq),
        in_specs=[pl.BlockSpec((None, LANES, tq), lambda bi, hi, qi: (bi, hi, qi)),
                  pl.BlockSpec((None, l, LANES), lambda bi, hi, qi: (bi, 0, hi)),
                  pl.BlockSpec((None, DA_DV, l), lambda bi, hi, qi: (bi, hi, 0)),
                  pl.BlockSpec((4, DA_DH), lambda bi, hi, qi: (0, 0)),
                  pl.BlockSpec((1, DA_DV), lambda bi, hi, qi: (0, 0))],
        out_specs=pl.BlockSpec((None, tq, DA_DV), lambda bi, hi, qi: (bi, qi, hi)),
        out_shape=jax.ShapeDtypeStruct((b, l, DA_V_W), BF16),
        scratch_shapes=[pltpu.VMEM((2, 1, tq), F32),
                        pltpu.VMEM((2, 1, tq), F32),
                        pltpu.VMEM((2, DA_DV, tq), F32),
                        pltpu.VMEM((2, tq, tq), F32), pltpu.VMEM((2, 1, tq), F32),
                        pltpu.VMEM((2, tq, tq), F32), pltpu.VMEM((2, 1, tq), F32)],
        compiler_params=_cparams(("parallel", "parallel", "arbitrary")),
        name="diff_attn",
    )(qt, kr, vt, lam_p, subln_g)


def _merge_kernel(o0_ref, o1_ref, o2_ref, g0_ref, g1_ref, g2_ref, wb_ref, wo_ref, h_ref, out_ref):
    y = None
    for n, (o_ref, g_ref) in enumerate(((o0_ref, g0_ref), (o1_ref, g1_ref), (o2_ref, g2_ref))):
        term = _sigmoid(g_ref[...]) * _dot(o_ref[...], wb_ref[n])
        y = term if y is None else y + term
    out_ref[...] = h_ref[...] + _dot(y.astype(BF16), wo_ref[...])


def _merge(o_ret, o_hg, o_da, proj2, wb, wo, h2, tm=256):
    t, d = h2.shape
    row = lambda i: (i, 0)
    const2 = lambda i: (0, 0)
    return pl.pallas_call(
        _merge_kernel,
        grid=(t // tm,),
        in_specs=[pl.BlockSpec((tm, d), row), pl.BlockSpec((tm, d), row), pl.BlockSpec((tm, d), row),
                  pl.BlockSpec((tm, d), lambda i: (i, OFF_MG // D_MODEL)),
                  pl.BlockSpec((tm, d), lambda i: (i, OFF_MG // D_MODEL + 1)),
                  pl.BlockSpec((tm, d), lambda i: (i, OFF_MG // D_MODEL + 2)),
                  pl.BlockSpec((N_BRANCH, d, d), lambda i: (0, 0, 0)),
                  pl.BlockSpec((d, d), const2),
                  pl.BlockSpec((tm, d), row)],
        out_specs=pl.BlockSpec((tm, d), row),
        out_shape=jax.ShapeDtypeStruct((t, d), F32),
        compiler_params=_cparams(("parallel",)),
        name="merge",
    )(o_ret, o_hg, o_da, proj2, proj2, proj2, wb, wo, h2)


FFN_HALO = 8


def _ffn_kernel(xp_ref, x_ref, g_ref, wg_ref, wv_ref, cwg_ref, cwv_ref, cbg_ref, cbv_ref, wo_ref,
                out_ref, hn_ref, acc_ref, *, tm, seq):
    i = pl.program_id(0)
    j = pl.program_id(1)

    @pl.when(j == 0)
    def _():
        x = jnp.concatenate([xp_ref[...], x_ref[...]], axis=0)
        hn_ref[...] = (_rms_rows(x) * g_ref[...]).astype(BF16)
        acc_ref[...] = x_ref[...]

    hn = hn_ref[...]
    rows = i * tm - FFN_HALO + lax.broadcasted_iota(jnp.int32, (tm + FFN_HALO, 1), 0)
    valid = lax.rem(rows + seq, seq) >= FIRST_VALID

    def conv(w_ref, cw_ref, cb_ref):
        u = jnp.where(valid, _dot(hn, w_ref[...]), 0.0)
        cw = cw_ref[...]
        return (cb_ref[...] + cw[0:1] * pltpu.roll(u, 2, 0)[FFN_HALO:]
                + cw[1:2] * pltpu.roll(u, 1, 0)[FFN_HALO:] + cw[2:3] * u[FFN_HALO:])

    act = _silu(conv(wg_ref, cwg_ref, cbg_ref)) * conv(wv_ref, cwv_ref, cbv_ref)
    acc_ref[...] += _dot(act.astype(BF16), wo_ref[...])

    @pl.when(j == pl.num_programs(1) - 1)
    def _():
        out_ref[...] = acc_ref[...]


def _ffn(h2, g, w_in, conv_w, conv_b, w_out, seq, tm=1280, fk=256):
    t, d = h2.shape
    nf = D_FF // fk
    hb = tm // FFN_HALO
    return pl.pallas_call(
        functools.partial(_ffn_kernel, tm=tm, seq=seq),
        grid=(t // tm, nf),
        in_specs=[pl.BlockSpec((FFN_HALO, d), lambda i, j: (jnp.maximum(i * hb - 1, 0), 0)),
                  pl.BlockSpec((tm, d), lambda i, j: (i, 0)),
                  pl.BlockSpec((1, d), lambda i, j: (0, 0)),
                  pl.BlockSpec((d, fk), lambda i, j: (0, j)),
                  pl.BlockSpec((d, fk), lambda i, j: (0, nf + j)),
                  pl.BlockSpec((CONV_W, fk), lambda i, j: (0, j)),
                  pl.BlockSpec((CONV_W, fk), lambda i, j: (0, nf + j)),
                  pl.BlockSpec((1, fk), lambda i, j: (0, j)),
                  pl.BlockSpec((1, fk), lambda i, j: (0, nf + j)),
                  pl.BlockSpec((fk, d), lambda i, j: (j, 0))],
        out_specs=pl.BlockSpec((tm, d), lambda i, j: (i, 0)),
        out_shape=jax.ShapeDtypeStruct((t, d), F32),
        scratch_shapes=[pltpu.VMEM((tm + FFN_HALO, d), BF16), pltpu.VMEM((tm, d), F32)],
        compiler_params=_cparams(("parallel", "arbitrary")),
        name="ffn",
    )(h2, h2, g, w_in, w_in, conv_w, conv_w, conv_b, conv_b, w_out)


def _final_kernel(x_ref, g_ref, o_ref):
    o_ref[...] = _rms_rows(x_ref[...]) * g_ref[...]


def _final_norm(h, g):
    b, l, d = h.shape
    s = l - PAD
    return pl.pallas_call(
        _final_kernel,
        grid=(b, s // PAD),
        in_specs=[pl.BlockSpec((None, PAD, d), lambda bi, i: (bi, i + 1, 0)),
                  pl.BlockSpec((1, d), lambda bi, i: (0, 0))],
        out_specs=pl.BlockSpec((None, PAD, d), lambda bi, i: (bi, i, 0)),
        out_shape=jax.ShapeDtypeStruct((b, s, d), F32),
        compiler_params=_cparams(("parallel", "parallel")),
        name="final_norm",
    )(h, g)


def _rope_tables(pos, d, reps):
    inv = ROPE_THETA ** (-jnp.arange(0, d, 2, dtype=F32) / d)
    ang = pos.astype(F32)[:, None] * inv[None, :]
    cos = jnp.cos(ang)
    sin = jnp.sin(ang)
    return jnp.tile(jnp.concatenate([cos, cos], axis=1), (1, reps)), jnp.tile(jnp.concatenate([-sin, sin], axis=1), (1, reps))


def _retention_tables():
    log_g = jnp.log1p(-jnp.exp2(-5.0 - jnp.arange(RET_HEADS, dtype=F32)))
    idx = jnp.arange(CHUNK, dtype=F32)
    gap = idx[:, None] - idx[None, :]
    intra = jnp.where(gap >= 0, jnp.exp(log_g[:, None, None] * jnp.maximum(gap, 0.0)), 0.0)
    q_dec = jnp.exp(log_g[:, None] * (idx[None, :] + 1.0))
    k_dec = jnp.exp(log_g[:, None] * (CHUNK - 1.0 - idx[None, :]))
    c_dec = jnp.exp(log_g * CHUNK)
    qd = jnp.broadcast_to(q_dec[:, :, None], (RET_HEADS, CHUNK, RET_DK))
    kd = jnp.broadcast_to(k_dec[:, :, None], (RET_HEADS, CHUNK, RET_DK))
    cd = jnp.broadcast_to(c_dec[:, None, None], (RET_HEADS, 1, RET_DV))
    return intra, qd, kd, cd


def kernel(x, meta, norm_mix_g, w_in, w_branch, w_out, hg_lb, da_lambda, da_subln_g, norm_ffn_g,
           w_ffn_in, ffn_conv_w, ffn_conv_b, w_ffn_out, norm_final_g):
    b, s, d = x.shape
    l = PAD + s
    h = jnp.concatenate([jnp.zeros((b, PAD - N_META, d), x.dtype),
                         jnp.broadcast_to(meta[None].astype(x.dtype), (b, N_META, d)), x], axis=1)
    pos = jnp.arange(l) - FIRST_VALID
    lb_soft = jax.nn.softmax(hg_lb.astype(F32), axis=0)
    lbs = jnp.cumsum(lb_soft, axis=0) - lb_soft[0]
    ret_cos, ret_sin = _rope_tables(pos, RET_DK, 1)
    da_cos, da_sin = _rope_tables(pos, DA_DH, 2)
    intra, qd, kd, cd = _retention_tables()

    h2 = h.reshape(b * l, d)
    for li in range(DEPTH):
        lambda_init = 0.8 - 0.6 * math.exp(-0.3 * li)
        proj2 = _inproj(h2, norm_mix_g[li][None], w_in[li].astype(BF16))
        proj = proj2.reshape(b, l, IN_WIDTH)
        o_ret = _retention(proj, ret_cos, ret_sin, intra, qd, kd, cd)
        o_hg = _hgrn2(proj, lbs[li].reshape(HG_HEADS, 1, HG_DK))
        qt, kr, vt = _da_prep(proj, da_cos, da_sin)
        o_da = _diff_attn(qt, kr, vt, da_lambda[li].astype(F32), da_subln_g[li].astype(F32)[None], lambda_init)
        h2 = _merge(o_ret.reshape(b * l, -1), o_hg.reshape(b * l, -1), o_da.reshape(b * l, -1), proj2,
                    w_branch[li].astype(BF16), w_out[li].astype(BF16), h2)
        h2 = _ffn(h2, norm_ffn_g[li][None], w_ffn_in[li].astype(BF16), ffn_conv_w[li], ffn_conv_b[li][None],
                  w_ffn_out[li].astype(BF16), l)
    return _final_norm(h2.reshape(b, l, d), norm_final_g[None])
```

```python
import functools
import math

import jax
import jax.numpy as jnp
from jax import lax
from jax.experimental import pallas as pl
from jax.experimental.pallas import tpu as pltpu

D_MODEL = 1024
DEPTH = 2
N_META = 16
CHUNK = 128
PAD = CHUNK
FIRST_VALID = PAD - N_META
ROPE_THETA = 10000.0
EPS = 1e-6
LB_FLOOR = 1e-30
RET_HEADS = 4
RET_DK = 128
RET_DV = 256
HG_HEADS = 8
HG_DK = 128
HG_DV = 128
DA_HEADS = 8
DA_DH = 64
DA_DV = 2 * DA_DH
N_BRANCH = 3
D_FF = 2816
CONV_W = 3
MASK_VALUE = -1e30
LOG2E = 1.4426950408889634
HG_LEVELS = 7

RET_QK_W = RET_HEADS * RET_DK
RET_V_W = RET_HEADS * RET_DV
HG_K_W = HG_HEADS * HG_DK
HG_V_W = HG_HEADS * HG_DV
DA_QK_W = DA_HEADS * 2 * DA_DH
DA_V_W = DA_HEADS * DA_DV
IN_SPLITS = (RET_QK_W, RET_QK_W, RET_V_W, RET_V_W, HG_K_W, HG_K_W, HG_V_W, HG_V_W,
             DA_QK_W, DA_QK_W, DA_V_W, N_BRANCH * D_MODEL)
IN_WIDTH = sum(IN_SPLITS)
(OFF_RQ, OFF_RK, OFF_RV, OFF_RG, OFF_HQ, OFF_HF, OFF_HI, OFF_HG,
 OFF_DQ, OFF_DK, OFF_DV, OFF_MG) = [sum(IN_SPLITS[:i]) for i in range(len(IN_SPLITS))]

F32 = jnp.float32
BF16 = jnp.bfloat16
LANES = 128
VMEM_LIMIT = 56 * 1024 * 1024


def _cparams(sem):
    return pltpu.CompilerParams(dimension_semantics=sem, vmem_limit_bytes=VMEM_LIMIT)


def _dot(a, b):
    return jnp.dot(a, b, preferred_element_type=F32)


def _dot_nt(a, b):
    return lax.dot_general(a, b, (((1,), (1,)), ((), ())), preferred_element_type=F32)


def _silu(x):
    return x * (1.0 / (1.0 + jnp.exp(-x)))


def _sigmoid(x):
    return 1.0 / (1.0 + jnp.exp(-x))


def _rms_rows(x):
    return x * lax.rsqrt(jnp.mean(x * x, axis=-1, keepdims=True) + EPS)


def _inproj_kernel(x_ref, g_ref, w_ref, o_ref, hn_ref):
    @pl.when(pl.program_id(1) == 0)
    def _():
        hn_ref[...] = (_rms_rows(x_ref[...]) * g_ref[...]).astype(BF16)

    o_ref[...] = _dot(hn_ref[...], w_ref[...]).astype(o_ref.dtype)


def _inproj(h2, g, w, tm=1280, tn=1024):
    t, d = h2.shape
    n = w.shape[1]
    return pl.pallas_call(
        _inproj_kernel,
        grid=(t // tm, n // tn),
        in_specs=[pl.BlockSpec((tm, d), lambda i, j: (i, 0)),
                  pl.BlockSpec((1, d), lambda i, j: (0, 0)),
                  pl.BlockSpec((d, tn), lambda i, j: (0, j))],
        out_specs=pl.BlockSpec((tm, tn), lambda i, j: (i, j)),
        out_shape=jax.ShapeDtypeStruct((t, n), BF16),
        scratch_shapes=[pltpu.VMEM((tm, d), BF16)],
        compiler_params=_cparams(("parallel", "arbitrary")),
        name="inproj",
    )(h2, g, w)


def _ret_kernel(q_ref, k_ref, v_ref, g_ref, cos_ref, sin_ref, intra_ref, qd_ref, kd_ref, cd_ref,
                o_ref, st_ref, *, tl, nb):
    lblk = pl.program_id(1)

    @pl.when(lblk == 0)
    def _():
        st_ref[...] = jnp.zeros_like(st_ref)

    intra = intra_ref[...]
    qd = qd_ref[...]
    kd = kd_ref[...]
    cd = cd_ref[...]
    row = lax.broadcasted_iota(jnp.int32, (CHUNK, RET_DK), 0)

    def chunk(bi, r, cos, sin):
        rows = pl.ds(r, CHUNK)
        q = q_ref[bi, rows, :].astype(F32)
        k = k_ref[bi, rows, :].astype(F32)
        qr = q * cos + pltpu.roll(q, RET_DK // 2, 1) * sin
        kr = (k * cos + pltpu.roll(k, RET_DK // 2, 1) * sin) * RET_DK ** -0.5
        kr = jnp.where(lblk * tl + r + row >= FIRST_VALID, kr, 0.0)
        v = v_ref[bi, rows, :].astype(BF16)
        state = st_ref[bi]
        s = _dot_nt(qr.astype(BF16), kr.astype(BF16)) * intra
        o = _dot(s.astype(BF16), v) + _dot((qr * qd).astype(BF16), state.astype(BF16))
        st_ref[bi] = cd * state + _dot((kr * kd).T.astype(BF16), v)
        o_ref[bi, rows, :] = (_rms_rows(o) * _silu(g_ref[bi, rows, :].astype(F32))).astype(o_ref.dtype)

    def body(c, carry):
        r = pl.multiple_of(c * CHUNK, CHUNK)
        cos = cos_ref[pl.ds(r, CHUNK), :]
        sin = sin_ref[pl.ds(r, CHUNK), :]
        for bi in range(nb):
            chunk(bi, r, cos, sin)
        return carry

    lax.fori_loop(0, tl // CHUNK, body, 0)


def _retention(proj, cos, sin, intra, qd, kd, cd, tl=1664):
    b, l, _ = proj.shape

    def col(off, w):
        return lambda hi, li: (0, li, off // w + hi)

    return pl.pallas_call(
        functools.partial(_ret_kernel, tl=tl, nb=b),
        grid=(RET_HEADS, l // tl),
        in_specs=[pl.BlockSpec((b, tl, RET_DK), col(OFF_RQ, RET_DK)),
                  pl.BlockSpec((b, tl, RET_DK), col(OFF_RK, RET_DK)),
                  pl.BlockSpec((b, tl, RET_DV), col(OFF_RV, RET_DV)),
                  pl.BlockSpec((b, tl, RET_DV), col(OFF_RG, RET_DV)),
                  pl.BlockSpec((tl, RET_DK), lambda hi, li: (li, 0)),
                  pl.BlockSpec((tl, RET_DK), lambda hi, li: (li, 0)),
                  pl.BlockSpec((None, CHUNK, CHUNK), lambda hi, li: (hi, 0, 0)),
                  pl.BlockSpec((None, CHUNK, RET_DK), lambda hi, li: (hi, 0, 0)),
                  pl.BlockSpec((None, CHUNK, RET_DK), lambda hi, li: (hi, 0, 0)),
                  pl.BlockSpec((None, 1, RET_DV), lambda hi, li: (hi, 0, 0))],
        out_specs=pl.BlockSpec((b, tl, RET_DV), lambda hi, li: (0, li, hi)),
        out_shape=jax.ShapeDtypeStruct((b, l, RET_V_W), BF16),
        scratch_shapes=[pltpu.VMEM((b, RET_DK, RET_DV), F32)],
        compiler_params=_cparams(("parallel", "arbitrary")),
        name="retention",
    )(proj, proj, proj, proj, cos, sin, intra, qd, kd, cd)


def _hg_kernel(q_ref, f_ref, i_ref, g_ref, lb_ref, o_ref, st_ref, cb_sc, *, tl, nb):
    lblk = pl.program_id(1)

    @pl.when(lblk == 0)
    def _():
        st_ref[...] = jnp.zeros_like(st_ref)

    lb = lb_ref[...]
    log_lb = jnp.log(jnp.maximum(lb, LB_FLOOR))
    log1m_lb = jnp.log1p(-lb)
    one_m_lb = 1.0 - lb
    row = lax.broadcasted_iota(jnp.int32, (CHUNK, HG_DK), 0)
    lane = lax.broadcasted_iota(jnp.int32, (CHUNK, HG_DK), 1)
    tri = (row >= lane).astype(F32)
    diff_bits = row ^ lane
    level = jnp.where(row == lane, -1, HG_LEVELS)
    for lv in range(HG_LEVELS):
        level = jnp.where((row > lane) & ((diff_bits >> lv) == 1), lv, level)
    half_row = lax.broadcasted_iota(jnp.int32, (8, HG_DK), 0) < 4
    odd_row = (row & 1) == 1

    def boundary_decay(cb, cb_ref, lv):
        s = 1 << lv
        if s == 1:
            ref_pt = jnp.where(odd_row, pltpu.roll(cb, 1, 0), cb)
        elif s == 2:
            ref_pt = jnp.concatenate(
                [jnp.where(half_row, jnp.broadcast_to(cb_ref[8 * v + 1:8 * v + 2, :], (8, HG_DK)),
                           jnp.broadcast_to(cb_ref[8 * v + 5:8 * v + 6, :], (8, HG_DK)))
                 for v in range(CHUNK // 8)], axis=0)
        else:
            ref_pt = jnp.concatenate(
                [jnp.broadcast_to(cb_ref[p * 2 * s + s - 1:p * 2 * s + s, :], (2 * s, HG_DK))
                 for p in range(CHUNK // (2 * s))], axis=0)
        return jnp.exp(-jnp.abs(cb - ref_pt))

    def chunk(bi, r):
        rows = pl.ds(r, CHUNK)
        cb_ref = cb_sc.at[bi]
        z = f_ref[bi, rows, :].astype(F32)
        q = q_ref[bi, rows, :].astype(F32)
        log_sig = jnp.minimum(z, 0.0) - jnp.log1p(jnp.exp(-jnp.abs(z)))
        bb = log1m_lb + log_sig
        log_f = jnp.maximum(log_lb, bb) + jnp.log1p(jnp.exp(-jnp.abs(log_lb - bb)))
        k = one_m_lb * (1.0 / (1.0 + jnp.exp(z)))
        v = jnp.where(lblk * tl + r + row >= FIRST_VALID, i_ref[bi, rows, :].astype(F32), 0.0).astype(BF16)
        cb = jnp.dot(tri, log_f, precision=lax.Precision.HIGHEST, preferred_element_type=F32)
        cb_ref[...] = cb
        state = st_ref[bi]
        cend = cb_ref[CHUNK - 1:CHUNK, :]
        o = _dot((q * jnp.exp(cb)).astype(BF16), state.astype(BF16))
        kdec = k * jnp.exp(cend - cb)
        dec_col = jnp.broadcast_to(jnp.exp(cend), (HG_DK, HG_DK)).T
        st_ref[bi] = dec_col * state + _dot(kdec.T.astype(BF16), v)

        amat = jnp.where(level == -1, _dot_nt(q.astype(BF16), k.astype(BF16)), 0.0)
        for lv in range(HG_LEVELS):
            w = boundary_decay(cb, cb_ref, lv)
            amat = jnp.where(level == lv, _dot_nt((q * w).astype(BF16), (k * w).astype(BF16)), amat)
        o = o + _dot(amat.astype(BF16), v)
        o_ref[bi, rows, :] = (_rms_rows(o) * _silu(g_ref[bi, rows, :].astype(F32))).astype(o_ref.dtype)

    def body(c, carry):
        r = pl.multiple_of(c * CHUNK, CHUNK)
        for bi in range(nb):
            chunk(bi, r)
        return carry

    lax.fori_loop(0, tl // CHUNK, body, 0)


def _hgrn2(proj, lb, tl=1664):
    b, l, _ = proj.shape

    def col(off):
        return lambda hi, li: (0, li, off // HG_DK + hi)

    return pl.pallas_call(
        functools.partial(_hg_kernel, tl=tl, nb=b),
        grid=(HG_HEADS, l // tl),
        in_specs=[pl.BlockSpec((b, tl, HG_DK), col(OFF_HQ)),
                  pl.BlockSpec((b, tl, HG_DK), col(OFF_HF)),
                  pl.BlockSpec((b, tl, HG_DV), col(OFF_HI)),
                  pl.BlockSpec((b, tl, HG_DV), col(OFF_HG)),
                  pl.BlockSpec((None, 1, HG_DK), lambda hi, li: (hi, 0, 0))],
        out_specs=pl.BlockSpec((b, tl, HG_DV), lambda hi, li: (0, li, hi)),
        out_shape=jax.ShapeDtypeStruct((b, l, HG_V_W), BF16),
        scratch_shapes=[pltpu.VMEM((b, HG_DK, HG_DV), F32),
                        pltpu.VMEM((b, CHUNK, HG_DK), F32)],
        compiler_params=_cparams(("parallel", "arbitrary")),
        name="hgrn2",
    )(proj, proj, proj, proj, lb)


def _da_prep_kernel(q_ref, k_ref, v_ref, cos_ref, sin_ref, qo_ref, kt_ref, vo_ref):
    cos = cos_ref[...]
    sin = sin_ref[...]
    lane = lax.broadcasted_iota(jnp.int32, cos.shape, 1)
    first_half = (lane % DA_DH) < DA_DH // 2

    def rope(x):
        swapped = jnp.where(first_half, pltpu.roll(x, LANES - DA_DH // 2, 1), pltpu.roll(x, DA_DH // 2, 1))
        return x * cos + swapped * sin

    qo_ref[...] = (rope(q_ref[...].astype(F32)) * (DA_DH ** -0.5 * LOG2E)).T.astype(BF16)
    kt_ref[...] = rope(k_ref[...].astype(F32)).astype(BF16)
    vo_ref[...] = v_ref[...].astype(F32).T.astype(BF16)


def _da_prep(proj, cos, sin, tl=640):
    b, l, _ = proj.shape

    def col(off):
        return lambda bi, hi, li: (bi, li, off // LANES + hi)

    return pl.pallas_call(
        _da_prep_kernel,
        grid=(b, DA_HEADS, l // tl),
        in_specs=[pl.BlockSpec((None, tl, LANES), col(OFF_DQ)),
                  pl.BlockSpec((None, tl, LANES), col(OFF_DK)),
                  pl.BlockSpec((None, tl, LANES), col(OFF_DV)),
                  pl.BlockSpec((tl, LANES), lambda bi, hi, li: (li, 0)),
                  pl.BlockSpec((tl, LANES), lambda bi, hi, li: (li, 0))],
        out_specs=[pl.BlockSpec((None, LANES, tl), lambda bi, hi, li: (bi, hi, li)),
                   pl.BlockSpec((None, tl, LANES), lambda bi, hi, li: (bi, li, hi)),
                   pl.BlockSpec((None, LANES, tl), lambda bi, hi, li: (bi, hi, li))],
        out_shape=[jax.ShapeDtypeStruct((b, DA_QK_W, l), BF16),
                   jax.ShapeDtypeStruct((b, l, DA_QK_W), BF16),
                   jax.ShapeDtypeStruct((b, DA_V_W, l), BF16)],
        compiler_params=_cparams(("parallel", "parallel", "parallel")),
        name="da_prep",
    )(proj, proj, proj, cos, sin)


def _da_kernel(qt_ref, k_ref, vt_ref, lam_ref, sg_ref, o_ref, m_ref, l_ref, acc_ref, sa_ref, ba_ref, sb_ref, bb_ref,
               *, tq, lambda_init):
    qi = pl.program_id(2)
    qt = qt_ref[...]
    row = lax.broadcasted_iota(jnp.int32, qt.shape, 0)
    zero = jnp.zeros_like(qt)
    qmaps = (jnp.where(row < DA_DH, qt, zero), jnp.where(row >= DA_DH, qt, zero))
    m_ref[...] = jnp.full_like(m_ref, MASK_VALUE)
    l_ref[...] = jnp.zeros_like(l_ref)
    acc_ref[...] = jnp.zeros_like(acc_ref)
    kloc = lax.broadcasted_iota(jnp.int32, (tq, tq), 0)
    qpos = qi * tq + lax.broadcasted_iota(jnp.int32, (tq, tq), 1)

    bufs = ((sa_ref, ba_ref), (sb_ref, bb_ref))

    def block_rows(kb):
        return pl.ds(pl.multiple_of(kb * tq, tq), tq)

    def scores(kb, allowed, buf):
        s_ref, bmax_ref = buf
        k = k_ref[block_rows(kb), :]
        for m in range(2):
            s = _dot(k, qmaps[m])
            if allowed is not None:
                s = jnp.where(allowed, s, MASK_VALUE)
            s_ref[m] = s
            bmax_ref[m] = jnp.max(s, axis=0, keepdims=True)

    def accumulate(kb, buf):
        s_ref, bmax_ref = buf
        vt = vt_ref[:, block_rows(kb)]
        for m in range(2):
            m_prev = m_ref[m]
            m_new = jnp.maximum(m_prev, bmax_ref[m])
            p = jnp.exp2(s_ref[m] - m_new)
            alpha = jnp.exp2(m_prev - m_new)
            l_ref[m] = alpha * l_ref[m] + jnp.sum(p, axis=0, keepdims=True)
            acc_ref[m] = alpha * acc_ref[m] + _dot(vt, p.astype(BF16))
            m_ref[m] = m_new

    scores(0, (kloc <= qpos) & (kloc >= FIRST_VALID), bufs[0])

    n_mid = jnp.maximum(qi - 1, 0)

    def pair(t, carry):
        j = 2 * t
        scores(j + 1, None, bufs[1])
        accumulate(j, bufs[0])
        scores(j + 2, None, bufs[0])
        accumulate(j + 1, bufs[1])
        return carry

    lax.fori_loop(0, n_mid // 2, pair, 0)
    odd = lax.rem(n_mid, 2) == 1

    @pl.when(odd)
    def _():
        scores(n_mid, None, bufs[1])
        accumulate(n_mid - 1, bufs[0])

    causal = kloc + qi * tq <= qpos
    for parity in (0, 1):
        @pl.when((qi > 0) & (odd == (parity == 1)))
        def _():
            scores(qi, causal, bufs[1 - parity])
            accumulate(qi - 1, bufs[parity])
            accumulate(qi, bufs[1 - parity])

    @pl.when(qi == 0)
    def _():
        accumulate(0, bufs[0])

    lp = lam_ref[...]
    lam = (jnp.exp(jnp.sum(lp[0:1] * lp[1:2], axis=1, keepdims=True))
           - jnp.exp(jnp.sum(lp[2:3] * lp[3:4], axis=1, keepdims=True)) + lambda_init)
    ot = acc_ref[0] / l_ref[0] - lam * (acc_ref[1] / l_ref[1])
    o_ref[...] = (_rms_rows(ot.T) * sg_ref[...] * (1.0 - lambda_init)).astype(o_ref.dtype)


def _diff_attn(qt, kr, vt, lam_p, subln_g, lambda_init, tq=640):
    b, l, _ = kr.shape
    return pl.pallas_call(
        functools.partial(_da_kernel, tq=tq, lambda_init=lambda_init),
        grid=(b, DA_HEADS, l // tq),
        in_specs=[pl.BlockSpec((None, LANES, tq), lambda bi, hi, qi: (bi, hi, qi)),
                  pl.BlockSpec((None, l, LANES), lambda bi, hi, qi: (bi, 0, hi)),
                  pl.BlockSpec((None, DA_DV, l), lambda bi, hi, qi: (bi, hi, 0)),
                  pl.BlockSpec((4, DA_DH), lambda bi, hi, qi: (0, 0)),
                  pl.BlockSpec((1, DA_DV), lambda bi, hi, qi: (0, 0))],
        out_specs=pl.BlockSpec((None, tq, DA_DV), lambda bi, hi, qi: (bi, qi, hi)),
        out_shape=jax.ShapeDtypeStruct((b, l, DA_V_W), BF16),
        scratch_shapes=[pltpu.VMEM((2, 1, tq), F32),
                        pltpu.VMEM((2, 1, tq), F32),
                        pltpu.VMEM((2, DA_DV, tq), F32),
                        pltpu.VMEM((2, tq, tq), F32), pltpu.VMEM((2, 1, tq), F32),
                        pltpu.VMEM((2, tq, tq), F32), pltpu.VMEM((2, 1, tq), F32)],
        compiler_params=_cparams(("parallel", "parallel", "arbitrary")),
        name="diff_attn",
    )(qt, kr, vt, lam_p, subln_g)


def _merge_kernel(o0_ref, o1_ref, o2_ref, g0_ref, g1_ref, g2_ref, wb_ref, wo_ref, h_ref, out_ref):
    y = None
    for n, (o_ref, g_ref) in enumerate(((o0_ref, g0_ref), (o1_ref, g1_ref), (o2_ref, g2_ref))):
        term = _sigmoid(g_ref[...].astype(F32)) * _dot(o_ref[...], wb_ref[n])
        y = term if y is None else y + term
    out_ref[...] = h_ref[...] + _dot(y.astype(BF16), wo_ref[...])


def _merge(o_ret, o_hg, o_da, proj2, wb, wo, h2, tm=256):
    t, d = h2.shape
    row = lambda i: (i, 0)
    const2 = lambda i: (0, 0)
    return pl.pallas_call(
        _merge_kernel,
        grid=(t // tm,),
        in_specs=[pl.BlockSpec((tm, d), row), pl.BlockSpec((tm, d), row), pl.BlockSpec((tm, d), row),
                  pl.BlockSpec((tm, d), lambda i: (i, OFF_MG // D_MODEL)),
                  pl.BlockSpec((tm, d), lambda i: (i, OFF_MG // D_MODEL + 1)),
                  pl.BlockSpec((tm, d), lambda i: (i, OFF_MG // D_MODEL + 2)),
                  pl.BlockSpec((N_BRANCH, d, d), lambda i: (0, 0, 0)),
                  pl.BlockSpec((d, d), const2),
                  pl.BlockSpec((tm, d), row)],
        out_specs=pl.BlockSpec((tm, d), row),
        out_shape=jax.ShapeDtypeStruct((t, d), F32),
        compiler_params=_cparams(("parallel",)),
        name="merge",
    )(o_ret, o_hg, o_da, proj2, proj2, proj2, wb, wo, h2)


FFN_HALO = 8
FFN_FK = 256


def _ffn_kernel(xp_ref, x_ref, g_ref, win_ref, cw_ref, cb_ref, wo_ref, fg_ref, out_ref, act_ref,
                *, tm, seq, final):
    i = pl.program_id(0)
    x = jnp.concatenate([xp_ref[...], x_ref[...]], axis=0)
    rows = i * tm - FFN_HALO + lax.broadcasted_iota(jnp.int32, (tm + FFN_HALO, 1), 0)
    valid = lax.rem(rows + seq, seq) >= FIRST_VALID
    hn = jnp.where(valid, _rms_rows(x) * g_ref[...], 0.0).astype(BF16)

    def conv(off):
        cols = pl.ds(off, FFN_FK)
        u = _dot(hn, win_ref[:, cols])
        cw = cw_ref[:, cols]
        return (cb_ref[:, cols] + cw[0:1] * pltpu.roll(u, 2, 0)[FFN_HALO:]
                + cw[1:2] * pltpu.roll(u, 1, 0)[FFN_HALO:] + cw[2:3] * u[FFN_HALO:])

    for off in range(0, D_FF, FFN_FK):
        act = _silu(conv(off)) * conv(D_FF + off)
        act_ref[:, pl.ds(off, FFN_FK)] = act.astype(BF16)
    out = x_ref[...] + _dot(act_ref[...], wo_ref[...])
    if final:
        out = _rms_rows(out) * fg_ref[...]
    out_ref[...] = out


def _ffn(h2, g, w_in, conv_w, conv_b, w_out, final_g, seq, final, tm=256):
    t, d = h2.shape
    hb = tm // FFN_HALO
    resident = dict(pipeline_mode=pl.Buffered(1))
    return pl.pallas_call(
        functools.partial(_ffn_kernel, tm=tm, seq=seq, final=final),
        grid=(t // tm,),
        in_specs=[pl.BlockSpec((FFN_HALO, d), lambda i: (jnp.maximum(i * hb - 1, 0), 0)),
                  pl.BlockSpec((tm, d), lambda i: (i, 0)),
                  pl.BlockSpec((1, d), lambda i: (0, 0)),
                  pl.BlockSpec((d, 2 * D_FF), lambda i: (0, 0), **resident),
                  pl.BlockSpec((CONV_W, 2 * D_FF), lambda i: (0, 0)),
                  pl.BlockSpec((1, 2 * D_FF), lambda i: (0, 0)),
                  pl.BlockSpec((D_FF, d), lambda i: (0, 0), **resident),
                  pl.BlockSpec((1, d), lambda i: (0, 0))],
        out_specs=pl.BlockSpec((tm, d), lambda i: (i, 0)),
        out_shape=jax.ShapeDtypeStruct((t, d), F32),
        scratch_shapes=[pltpu.VMEM((tm, D_FF), BF16)],
        compiler_params=_cparams(("parallel",)),
        name="ffn",
    )(h2, h2, g, w_in, conv_w, conv_b, w_out, final_g)


def _rope_tables(pos, d, reps):
    inv = ROPE_THETA ** (-jnp.arange(0, d, 2, dtype=F32) / d)
    ang = pos.astype(F32)[:, None] * inv[None, :]
    cos = jnp.cos(ang)
    sin = jnp.sin(ang)
    return jnp.tile(jnp.concatenate([cos, cos], axis=1), (1, reps)), jnp.tile(jnp.concatenate([-sin, sin], axis=1), (1, reps))


def _retention_tables():
    log_g = jnp.log1p(-jnp.exp2(-5.0 - jnp.arange(RET_HEADS, dtype=F32)))
    idx = jnp.arange(CHUNK, dtype=F32)
    gap = idx[:, None] - idx[None, :]
    intra = jnp.where(gap >= 0, jnp.exp(log_g[:, None, None] * jnp.maximum(gap, 0.0)), 0.0)
    q_dec = jnp.exp(log_g[:, None] * (idx[None, :] + 1.0))
    k_dec = jnp.exp(log_g[:, None] * (CHUNK - 1.0 - idx[None, :]))
    c_dec = jnp.exp(log_g * CHUNK)
    qd = jnp.broadcast_to(q_dec[:, :, None], (RET_HEADS, CHUNK, RET_DK))
    kd = jnp.broadcast_to(k_dec[:, :, None], (RET_HEADS, CHUNK, RET_DK))
    cd = jnp.broadcast_to(c_dec[:, None, None], (RET_HEADS, 1, RET_DV))
    return intra, qd, kd, cd


def kernel(x, meta, norm_mix_g, w_in, w_branch, w_out, hg_lb, da_lambda, da_subln_g, norm_ffn_g,
           w_ffn_in, ffn_conv_w, ffn_conv_b, w_ffn_out, norm_final_g):
    b, s, d = x.shape
    l = PAD + s
    h = jnp.concatenate([jnp.zeros((b, PAD - N_META, d), x.dtype),
                         jnp.broadcast_to(meta[None].astype(x.dtype), (b, N_META, d)), x], axis=1)
    pos = jnp.arange(l) - FIRST_VALID
    lb_soft = jax.nn.softmax(hg_lb.astype(F32), axis=0)
    lbs = jnp.cumsum(lb_soft, axis=0) - lb_soft[0]
    ret_cos, ret_sin = _rope_tables(pos, RET_DK, 1)
    da_cos, da_sin = _rope_tables(pos, DA_DH, 2)
    intra, qd, kd, cd = _retention_tables()

    h2 = h.reshape(b * l, d)
    for li in range(DEPTH):
        lambda_init = 0.8 - 0.6 * math.exp(-0.3 * li)
        proj2 = _inproj(h2, norm_mix_g[li][None], w_in[li].astype(BF16))
        proj = proj2.reshape(b, l, IN_WIDTH)
        o_ret = _retention(proj, ret_cos, ret_sin, intra, qd, kd, cd)
        o_hg = _hgrn2(proj, lbs[li].reshape(HG_HEADS, 1, HG_DK))
        qt, kr, vt = _da_prep(proj, da_cos, da_sin)
        o_da = _diff_attn(qt, kr, vt, da_lambda[li].astype(F32), da_subln_g[li].astype(F32)[None], lambda_init)
        h2 = _merge(o_ret.reshape(b * l, -1), o_hg.reshape(b * l, -1), o_da.reshape(b * l, -1), proj2,
                    w_branch[li].astype(BF16), w_out[li].astype(BF16), h2)
        h2 = _ffn(h2, norm_ffn_g[li][None], w_ffn_in[li].astype(BF16), ffn_conv_w[li], ffn_conv_b[li][None],
                  w_ffn_out[li].astype(BF16), norm_final_g[None], l, final=li == DEPTH - 1)
    return h2.reshape(b, l, d)[:, PAD:]
```

```python
import functools
import math

import jax
import jax.numpy as jnp
from jax import lax
from jax.experimental import pallas as pl
from jax.experimental.pallas import tpu as pltpu

D_MODEL = 1024
DEPTH = 2
N_META = 16
CHUNK = 128
PAD = CHUNK
FIRST_VALID = PAD - N_META
ROPE_THETA = 10000.0
EPS = 1e-6
LB_FLOOR = 1e-30
RET_HEADS = 4
RET_DK = 128
RET_DV = 256
HG_HEADS = 8
HG_DK = 128
HG_DV = 128
DA_HEADS = 8
DA_DH = 64
DA_DV = 2 * DA_DH
N_BRANCH = 3
D_FF = 2816
CONV_W = 3
MASK_VALUE = -1e30
LOG2E = 1.4426950408889634
HG_LEVELS = 7
HG_GROUP = 2
RET_GROUP = 2

RET_QK_W = RET_HEADS * RET_DK
RET_V_W = RET_HEADS * RET_DV
HG_K_W = HG_HEADS * HG_DK
HG_V_W = HG_HEADS * HG_DV
DA_QK_W = DA_HEADS * 2 * DA_DH
DA_V_W = DA_HEADS * DA_DV
IN_SPLITS = (RET_QK_W, RET_QK_W, RET_V_W, RET_V_W, HG_K_W, HG_K_W, HG_V_W, HG_V_W,
             DA_QK_W, DA_QK_W, DA_V_W, N_BRANCH * D_MODEL)
IN_WIDTH = sum(IN_SPLITS)
(OFF_RQ, OFF_RK, OFF_RV, OFF_RG, OFF_HQ, OFF_HF, OFF_HI, OFF_HG,
 OFF_DQ, OFF_DK, OFF_DV, OFF_MG) = [sum(IN_SPLITS[:i]) for i in range(len(IN_SPLITS))]

F32 = jnp.float32
BF16 = jnp.bfloat16
LANES = 128
VMEM_LIMIT = 56 * 1024 * 1024


def _cparams(sem):
    return pltpu.CompilerParams(dimension_semantics=sem, vmem_limit_bytes=VMEM_LIMIT)


def _dot(a, b):
    return jnp.dot(a, b, preferred_element_type=F32)


def _dot_nt(a, b):
    return lax.dot_general(a, b, (((1,), (1,)), ((), ())), preferred_element_type=F32)


def _dot_tn(a, b):
    return lax.dot_general(a, b, (((0,), (0,)), ((), ())), preferred_element_type=F32)


def _silu(x):
    return x * (1.0 / (1.0 + jnp.exp(-x)))


def _sigmoid(x):
    return 1.0 / (1.0 + jnp.exp(-x))


def _rms_rows(x):
    return x * lax.rsqrt(jnp.mean(x * x, axis=-1, keepdims=True) + EPS)


def _inproj_kernel(x_ref, g_ref, w_ref, o_ref, hn_ref):
    @pl.when(pl.program_id(1) == 0)
    def _():
        hn_ref[...] = (_rms_rows(x_ref[...]) * g_ref[...]).astype(BF16)

    o_ref[...] = _dot(hn_ref[...], w_ref[...]).astype(o_ref.dtype)


def _inproj(h2, g, w, tm=1280, tn=1024):
    t, d = h2.shape
    n = w.shape[1]
    return pl.pallas_call(
        _inproj_kernel,
        grid=(t // tm, n // tn),
        in_specs=[pl.BlockSpec((tm, d), lambda i, j: (i, 0)),
                  pl.BlockSpec((1, d), lambda i, j: (0, 0)),
                  pl.BlockSpec((d, tn), lambda i, j: (0, j))],
        out_specs=pl.BlockSpec((tm, tn), lambda i, j: (i, j)),
        out_shape=jax.ShapeDtypeStruct((t, n), BF16),
        scratch_shapes=[pltpu.VMEM((tm, d), BF16)],
        compiler_params=_cparams(("parallel", "arbitrary")),
        name="inproj",
    )(h2, g, w)


def _ret_kernel(q_ref, k_ref, v_ref, g_ref, cos_ref, sin_ref, intra_ref, qd_ref, kd_ref, cd_ref,
                o_ref, st_ref, *, tl, nb):
    lblk = pl.program_id(1)

    @pl.when(lblk == 0)
    def _():
        st_ref[...] = jnp.zeros_like(st_ref)

    row = lax.broadcasted_iota(jnp.int32, (CHUNK, RET_DK), 0)

    def chunk(bi, hh, r, cos, sin):
        rows = pl.ds(r, CHUNK)
        kcols = slice(hh * RET_DK, (hh + 1) * RET_DK)
        vcols = slice(hh * RET_DV, (hh + 1) * RET_DV)
        ci = bi * RET_GROUP + hh
        q = q_ref[bi, rows, kcols].astype(F32)
        k = k_ref[bi, rows, kcols].astype(F32)
        qr = q * cos + pltpu.roll(q, RET_DK // 2, 1) * sin
        kr = (k * cos + pltpu.roll(k, RET_DK // 2, 1) * sin) * RET_DK ** -0.5
        kr = jnp.where(lblk * tl + r + row >= FIRST_VALID, kr, 0.0)
        v = v_ref[bi, rows, vcols].astype(BF16)
        state = st_ref[ci]
        s = _dot_nt(qr.astype(BF16), kr.astype(BF16)) * intra_ref[hh]
        o = _dot(s.astype(BF16), v) + _dot((qr * qd_ref[hh]).astype(BF16), state.astype(BF16))
        st_ref[ci] = cd_ref[hh] * state + _dot((kr * kd_ref[hh]).T.astype(BF16), v)
        o_ref[bi, rows, vcols] = (_rms_rows(o) * _silu(g_ref[bi, rows, vcols].astype(F32))).astype(o_ref.dtype)

    def body(c, carry):
        r = pl.multiple_of(c * CHUNK, CHUNK)
        cos = cos_ref[pl.ds(r, CHUNK), :]
        sin = sin_ref[pl.ds(r, CHUNK), :]
        for bi in range(nb):
            for hh in range(RET_GROUP):
                chunk(bi, hh, r, cos, sin)
        return carry

    lax.fori_loop(0, tl // CHUNK, body, 0)


def _retention(proj, cos, sin, intra, qd, kd, cd, tl=1664):
    b, l, _ = proj.shape
    kw = RET_GROUP * RET_DK
    vw = RET_GROUP * RET_DV

    def col(off, w):
        return lambda hi, li: (0, li, off // w + hi)

    def per_head(shape):
        return pl.BlockSpec((RET_GROUP,) + shape, lambda hi, li: (hi, 0, 0))

    return pl.pallas_call(
        functools.partial(_ret_kernel, tl=tl, nb=b),
        grid=(RET_HEADS // RET_GROUP, l // tl),
        in_specs=[pl.BlockSpec((b, tl, kw), col(OFF_RQ, kw)),
                  pl.BlockSpec((b, tl, kw), col(OFF_RK, kw)),
                  pl.BlockSpec((b, tl, vw), col(OFF_RV, vw)),
                  pl.BlockSpec((b, tl, vw), col(OFF_RG, vw)),
                  pl.BlockSpec((tl, RET_DK), lambda hi, li: (li, 0)),
                  pl.BlockSpec((tl, RET_DK), lambda hi, li: (li, 0)),
                  per_head((CHUNK, CHUNK)), per_head((CHUNK, RET_DK)), per_head((CHUNK, RET_DK)),
                  per_head((1, RET_DV))],
        out_specs=pl.BlockSpec((b, tl, vw), lambda hi, li: (0, li, hi)),
        out_shape=jax.ShapeDtypeStruct((b, l, RET_V_W), BF16),
        scratch_shapes=[pltpu.VMEM((b * RET_GROUP, RET_DK, RET_DV), F32)],
        compiler_params=_cparams(("parallel", "arbitrary")),
        name="retention",
    )(proj, proj, proj, proj, cos, sin, intra, qd, kd, cd)


def _hg_kernel(q_ref, f_ref, i_ref, g_ref, lb_ref, o_ref, st_ref, cb_sc, *, tl, nb):
    lblk = pl.program_id(1)

    @pl.when(lblk == 0)
    def _():
        st_ref[...] = jnp.zeros_like(st_ref)

    row = lax.broadcasted_iota(jnp.int32, (CHUNK, HG_DK), 0)
    lane = lax.broadcasted_iota(jnp.int32, (CHUNK, HG_DK), 1)
    tri = (row >= lane).astype(BF16)
    diff_bits = row ^ lane
    level = jnp.where(row == lane, -1, HG_LEVELS)
    for lv in range(HG_LEVELS):
        level = jnp.where((row > lane) & ((diff_bits >> lv) == 1), lv, level)
    half_row = lax.broadcasted_iota(jnp.int32, (8, HG_DK), 0) < 4
    odd_row = (row & 1) == 1

    def boundary_decay(cb, cb_ref, lv):
        s = 1 << lv
        if s == 1:
            ref_pt = jnp.where(odd_row, pltpu.roll(cb, 1, 0), cb)
        elif s == 2:
            ref_pt = jnp.concatenate(
                [jnp.where(half_row, jnp.broadcast_to(cb_ref[8 * v + 1:8 * v + 2, :], (8, HG_DK)),
                           jnp.broadcast_to(cb_ref[8 * v + 5:8 * v + 6, :], (8, HG_DK)))
                 for v in range(CHUNK // 8)], axis=0)
        else:
            ref_pt = jnp.concatenate(
                [jnp.broadcast_to(cb_ref[p * 2 * s + s - 1:p * 2 * s + s, :], (2 * s, HG_DK))
                 for p in range(CHUNK // (2 * s))], axis=0)
        return jnp.exp(-jnp.abs(cb - ref_pt))

    def chunk(bi, hh, r):
        rows = pl.ds(r, CHUNK)
        cols = slice(hh * HG_DK, (hh + 1) * HG_DK)
        ci = bi * HG_GROUP + hh
        cb_ref = cb_sc.at[ci]
        lb = lb_ref[hh]
        log_lb = jnp.log(jnp.maximum(lb, LB_FLOOR))
        log1m_lb = jnp.log1p(-lb)
        one_m_lb = 1.0 - lb
        z = f_ref[bi, rows, cols].astype(F32)
        q = q_ref[bi, rows, cols].astype(F32)
        log_sig = jnp.minimum(z, 0.0) - jnp.log1p(jnp.exp(-jnp.abs(z)))
        bb = log1m_lb + log_sig
        log_f = jnp.maximum(log_lb, bb) + jnp.log1p(jnp.exp(-jnp.abs(log_lb - bb)))
        k = one_m_lb * (1.0 / (1.0 + jnp.exp(z)))
        v = jnp.where(lblk * tl + r + row >= FIRST_VALID, i_ref[bi, rows, cols].astype(F32), 0.0).astype(BF16)
        f_hi = log_f.astype(BF16)
        rest = log_f - f_hi.astype(F32)
        f_mid = rest.astype(BF16)
        f_lo = (rest - f_mid.astype(F32)).astype(BF16)
        cb = _dot(tri, f_hi) + _dot(tri, f_mid) + _dot(tri, f_lo)
        cb_ref[...] = cb
        state = st_ref[ci]
        cend = cb_ref[CHUNK - 1:CHUNK, :]
        o = _dot((q * jnp.exp(cb)).astype(BF16), state.astype(BF16))
        kdec = k * jnp.exp(cend - cb)
        dec_col = jnp.broadcast_to(jnp.exp(cend), (HG_DK, HG_DK)).T
        st_ref[ci] = dec_col * state + _dot(kdec.T.astype(BF16), v)

        amat = jnp.where(level == -1, _dot_nt(q.astype(BF16), k.astype(BF16)), 0.0)
        for lv in range(HG_LEVELS):
            w = boundary_decay(cb, cb_ref, lv)
            amat = jnp.where(level == lv, _dot_nt((q * w).astype(BF16), (k * w).astype(BF16)), amat)
        o = o + _dot(amat.astype(BF16), v)
        o_ref[bi, rows, cols] = (_rms_rows(o) * _silu(g_ref[bi, rows, cols].astype(F32))).astype(o_ref.dtype)

    def body(c, carry):
        r = pl.multiple_of(c * CHUNK, CHUNK)
        for bi in range(nb):
            for hh in range(HG_GROUP):
                chunk(bi, hh, r)
        return carry

    lax.fori_loop(0, tl // CHUNK, body, 0)


def _hgrn2(proj, lb, tl=1664):
    b, l, _ = proj.shape
    gw = HG_GROUP * HG_DK

    def col(off):
        return lambda hi, li: (0, li, off // gw + hi)

    return pl.pallas_call(
        functools.partial(_hg_kernel, tl=tl, nb=b),
        grid=(HG_HEADS // HG_GROUP, l // tl),
        in_specs=[pl.BlockSpec((b, tl, gw), col(OFF_HQ)),
                  pl.BlockSpec((b, tl, gw), col(OFF_HF)),
                  pl.BlockSpec((b, tl, gw), col(OFF_HI)),
                  pl.BlockSpec((b, tl, gw), col(OFF_HG)),
                  pl.BlockSpec((HG_GROUP, 1, HG_DK), lambda hi, li: (hi, 0, 0))],
        out_specs=pl.BlockSpec((b, tl, gw), lambda hi, li: (0, li, hi)),
        out_shape=jax.ShapeDtypeStruct((b, l, HG_V_W), BF16),
        scratch_shapes=[pltpu.VMEM((b * HG_GROUP, HG_DK, HG_DV), F32),
                        pltpu.VMEM((b * HG_GROUP, CHUNK, HG_DK), F32)],
        compiler_params=_cparams(("parallel", "arbitrary")),
        name="hgrn2",
    )(proj, proj, proj, proj, lb)


def _da_rope(x, cos, sin):
    lane = lax.broadcasted_iota(jnp.int32, x.shape, 1)
    first_half = (lane % DA_DH) < DA_DH // 2
    swapped = jnp.where(first_half, pltpu.roll(x, x.shape[1] - DA_DH // 2, 1), pltpu.roll(x, DA_DH // 2, 1))
    return x * cos + swapped * sin


def _da_prep_kernel(k_ref, cos_ref, sin_ref, ko_ref):
    ko_ref[...] = _da_rope(k_ref[...].astype(F32), cos_ref[...], sin_ref[...]).astype(BF16)


def _da_prep(proj, cos, sin, tl=1664):
    b, l, _ = proj.shape
    return pl.pallas_call(
        _da_prep_kernel,
        grid=(b, l // tl),
        in_specs=[pl.BlockSpec((None, tl, DA_QK_W), lambda bi, li: (bi, li, OFF_DK // DA_QK_W)),
                  pl.BlockSpec((tl, DA_QK_W), lambda bi, li: (li, 0)),
                  pl.BlockSpec((tl, DA_QK_W), lambda bi, li: (li, 0))],
        out_specs=pl.BlockSpec((None, tl, DA_QK_W), lambda bi, li: (bi, li, 0)),
        out_shape=jax.ShapeDtypeStruct((b, l, DA_QK_W), BF16),
        compiler_params=_cparams(("parallel", "parallel")),
        name="da_prep",
    )(proj, cos, sin)


def _da_kernel(q_ref, cos_ref, sin_ref, k_ref, v_ref, lam_ref, sg_ref, o_ref, m_ref, l_ref, acc_ref,
               sa_ref, ba_ref, sb_ref, bb_ref, *, tq, lambda_init):
    qi = pl.program_id(2)
    q = _da_rope(q_ref[...].astype(F32), cos_ref[...], sin_ref[...]) * (DA_DH ** -0.5 * LOG2E)
    qt = q.T.astype(BF16)
    row = lax.broadcasted_iota(jnp.int32, qt.shape, 0)
    zero = jnp.zeros_like(qt)
    qmaps = (jnp.where(row < DA_DH, qt, zero), jnp.where(row >= DA_DH, qt, zero))
    m_ref[...] = jnp.full_like(m_ref, MASK_VALUE)
    l_ref[...] = jnp.zeros_like(l_ref)
    acc_ref[...] = jnp.zeros_like(acc_ref)
    kloc = lax.broadcasted_iota(jnp.int32, (tq, tq), 0)
    qpos = qi * tq + lax.broadcasted_iota(jnp.int32, (tq, tq), 1)

    bufs = ((sa_ref, ba_ref), (sb_ref, bb_ref))

    def block_rows(kb):
        return pl.ds(pl.multiple_of(kb * tq, tq), tq)

    def scores(kb, allowed, buf):
        s_ref, bmax_ref = buf
        k = k_ref[block_rows(kb), :]
        for m in range(2):
            s = _dot(k, qmaps[m])
            if allowed is not None:
                s = jnp.where(allowed, s, MASK_VALUE)
            s_ref[m] = s
            bmax_ref[m] = jnp.max(s, axis=0, keepdims=True)

    def accumulate(kb, buf):
        s_ref, bmax_ref = buf
        v = v_ref[block_rows(kb), :]
        for m in range(2):
            m_prev = m_ref[m]
            m_new = jnp.maximum(m_prev, bmax_ref[m])
            p = jnp.exp2(s_ref[m] - m_new)
            alpha = jnp.exp2(m_prev - m_new)
            l_ref[m] = alpha * l_ref[m] + jnp.sum(p, axis=0, keepdims=True)
            acc_ref[m] = alpha * acc_ref[m] + _dot_tn(v, p.astype(BF16))
            m_ref[m] = m_new

    scores(0, (kloc <= qpos) & (kloc >= FIRST_VALID), bufs[0])

    n_mid = jnp.maximum(qi - 1, 0)

    def pair(t, carry):
        j = 2 * t
        scores(j + 1, None, bufs[1])
        accumulate(j, bufs[0])
        scores(j + 2, None, bufs[0])
        accumulate(j + 1, bufs[1])
        return carry

    lax.fori_loop(0, n_mid // 2, pair, 0)
    odd = lax.rem(n_mid, 2) == 1

    @pl.when(odd)
    def _():
        scores(n_mid, None, bufs[1])
        accumulate(n_mid - 1, bufs[0])

    causal = kloc + qi * tq <= qpos
    for parity in (0, 1):
        @pl.when((qi > 0) & (odd == (parity == 1)))
        def _():
            scores(qi, causal, bufs[1 - parity])
            accumulate(qi - 1, bufs[parity])
            accumulate(qi, bufs[1 - parity])

    @pl.when(qi == 0)
    def _():
        accumulate(0, bufs[0])

    lp = lam_ref[...]
    lam = (jnp.exp(jnp.sum(lp[0:1] * lp[1:2], axis=1, keepdims=True))
           - jnp.exp(jnp.sum(lp[2:3] * lp[3:4], axis=1, keepdims=True)) + lambda_init)
    ot = acc_ref[0] / l_ref[0] - lam * (acc_ref[1] / l_ref[1])
    o_ref[...] = (_rms_rows(ot.T) * sg_ref[...] * (1.0 - lambda_init)).astype(o_ref.dtype)


def _diff_attn(proj, kr, cos, sin, lam_p, subln_g, lambda_init, tq=640):
    b, l, _ = kr.shape
    return pl.pallas_call(
        functools.partial(_da_kernel, tq=tq, lambda_init=lambda_init),
        grid=(b, DA_HEADS, l // tq),
        in_specs=[pl.BlockSpec((None, tq, LANES), lambda bi, hi, qi: (bi, qi, OFF_DQ // LANES + hi)),
                  pl.BlockSpec((tq, LANES), lambda bi, hi, qi: (qi, 0)),
                  pl.BlockSpec((tq, LANES), lambda bi, hi, qi: (qi, 0)),
                  pl.BlockSpec((None, l, LANES), lambda bi, hi, qi: (bi, 0, hi)),
                  pl.BlockSpec((None, l, DA_DV), lambda bi, hi, qi: (bi, 0, OFF_DV // DA_DV + hi)),
                  pl.BlockSpec((4, DA_DH), lambda bi, hi, qi: (0, 0)),
                  pl.BlockSpec((1, DA_DV), lambda bi, hi, qi: (0, 0))],
        out_specs=pl.BlockSpec((None, tq, DA_DV), lambda bi, hi, qi: (bi, qi, hi)),
        out_shape=jax.ShapeDtypeStruct((b, l, DA_V_W), BF16),
        scratch_shapes=[pltpu.VMEM((2, 1, tq), F32),
                        pltpu.VMEM((2, 1, tq), F32),
                        pltpu.VMEM((2, DA_DV, tq), F32),
                        pltpu.VMEM((2, tq, tq), F32), pltpu.VMEM((2, 1, tq), F32),
                        pltpu.VMEM((2, tq, tq), F32), pltpu.VMEM((2, 1, tq), F32)],
        compiler_params=_cparams(("parallel", "parallel", "arbitrary")),
        name="diff_attn",
    )(proj, cos, sin, kr, proj, lam_p, subln_g)


def _merge_kernel(o0_ref, o1_ref, o2_ref, g0_ref, g1_ref, g2_ref, wb_ref, wo_ref, h_ref, out_ref):
    y = None
    for n, (o_ref, g_ref) in enumerate(((o0_ref, g0_ref), (o1_ref, g1_ref), (o2_ref, g2_ref))):
        term = _sigmoid(g_ref[...].astype(F32)) * _dot(o_ref[...], wb_ref[n])
        y = term if y is None else y + term
    out_ref[...] = h_ref[...] + _dot(y.astype(BF16), wo_ref[...])


def _merge(o_ret, o_hg, o_da, proj2, wb, wo, h2, tm=256):
    t, d = h2.shape
    row = lambda i: (i, 0)
    const2 = lambda i: (0, 0)
    return pl.pallas_call(
        _merge_kernel,
        grid=(t // tm,),
        in_specs=[pl.BlockSpec((tm, d), row), pl.BlockSpec((tm, d), row), pl.BlockSpec((tm, d), row),
                  pl.BlockSpec((tm, d), lambda i: (i, OFF_MG // D_MODEL)),
                  pl.BlockSpec((tm, d), lambda i: (i, OFF_MG // D_MODEL + 1)),
                  pl.BlockSpec((tm, d), lambda i: (i, OFF_MG // D_MODEL + 2)),
                  pl.BlockSpec((N_BRANCH, d, d), lambda i: (0, 0, 0)),
                  pl.BlockSpec((d, d), const2),
                  pl.BlockSpec((tm, d), row)],
        out_specs=pl.BlockSpec((tm, d), row),
        out_shape=jax.ShapeDtypeStruct((t, d), F32),
        compiler_params=_cparams(("parallel",)),
        name="merge",
    )(o_ret, o_hg, o_da, proj2, proj2, proj2, wb, wo, h2)


FFN_HALO = 8
FFN_FK = 256


def _ffn_kernel(xp_ref, x_ref, g_ref, win_ref, cw_ref, cb_ref, wo_ref, fg_ref, out_ref, act_ref,
                *, tm, seq, final):
    i = pl.program_id(0)
    x = jnp.concatenate([xp_ref[...], x_ref[...]], axis=0)
    rows = i * tm - FFN_HALO + lax.broadcasted_iota(jnp.int32, (tm + FFN_HALO, 1), 0)
    valid = lax.rem(rows + seq, seq) >= FIRST_VALID
    hn = jnp.where(valid, _rms_rows(x) * g_ref[...], 0.0).astype(BF16)

    def conv(off):
        cols = pl.ds(off, FFN_FK)
        u = _dot(hn, win_ref[:, cols])
        cw = cw_ref[:, cols]
        return (cb_ref[:, cols] + cw[0:1] * pltpu.roll(u, 2, 0)[FFN_HALO:]
                + cw[1:2] * pltpu.roll(u, 1, 0)[FFN_HALO:] + cw[2:3] * u[FFN_HALO:])

    for off in range(0, D_FF, FFN_FK):
        act = _silu(conv(off)) * conv(D_FF + off)
        act_ref[:, pl.ds(off, FFN_FK)] = act.astype(BF16)
    out = x_ref[...] + _dot(act_ref[...], wo_ref[...])
    if final:
        out = _rms_rows(out) * fg_ref[...]
    out_ref[...] = out


def _ffn(h2, g, w_in, conv_w, conv_b, w_out, final_g, seq, final, tm=256):
    t, d = h2.shape
    hb = tm // FFN_HALO
    resident = dict(pipeline_mode=pl.Buffered(1))
    return pl.pallas_call(
        functools.partial(_ffn_kernel, tm=tm, seq=seq, final=final),
        grid=(t // tm,),
        in_specs=[pl.BlockSpec((FFN_HALO, d), lambda i: (jnp.maximum(i * hb - 1, 0), 0)),
                  pl.BlockSpec((tm, d), lambda i: (i, 0)),
                  pl.BlockSpec((1, d), lambda i: (0, 0)),
                  pl.BlockSpec((d, 2 * D_FF), lambda i: (0, 0), **resident),
                  pl.BlockSpec((CONV_W, 2 * D_FF), lambda i: (0, 0)),
                  pl.BlockSpec((1, 2 * D_FF), lambda i: (0, 0)),
                  pl.BlockSpec((D_FF, d), lambda i: (0, 0), **resident),
                  pl.BlockSpec((1, d), lambda i: (0, 0))],
        out_specs=pl.BlockSpec((tm, d), lambda i: (i, 0)),
        out_shape=jax.ShapeDtypeStruct((t, d), F32),
        scratch_shapes=[pltpu.VMEM((tm, D_FF), BF16)],
        compiler_params=_cparams(("parallel",)),
        name="ffn",
    )(h2, h2, g, w_in, conv_w, conv_b, w_out, final_g)


def _rope_tables(pos, d, reps):
    inv = ROPE_THETA ** (-jnp.arange(0, d, 2, dtype=F32) / d)
    ang = pos.astype(F32)[:, None] * inv[None, :]
    cos = jnp.cos(ang)
    sin = jnp.sin(ang)
    return jnp.tile(jnp.concatenate([cos, cos], axis=1), (1, reps)), jnp.tile(jnp.concatenate([-sin, sin], axis=1), (1, reps))


def _retention_tables():
    log_g = jnp.log1p(-jnp.exp2(-5.0 - jnp.arange(RET_HEADS, dtype=F32)))
    idx = jnp.arange(CHUNK, dtype=F32)
    gap = idx[:, None] - idx[None, :]
    intra = jnp.where(gap >= 0, jnp.exp(log_g[:, None, None] * jnp.maximum(gap, 0.0)), 0.0)
    q_dec = jnp.exp(log_g[:, None] * (idx[None, :] + 1.0))
    k_dec = jnp.exp(log_g[:, None] * (CHUNK - 1.0 - idx[None, :]))
    c_dec = jnp.exp(log_g * CHUNK)
    qd = jnp.broadcast_to(q_dec[:, :, None], (RET_HEADS, CHUNK, RET_DK))
    kd = jnp.broadcast_to(k_dec[:, :, None], (RET_HEADS, CHUNK, RET_DK))
    cd = jnp.broadcast_to(c_dec[:, None, None], (RET_HEADS, 1, RET_DV))
    return intra, qd, kd, cd


def kernel(x, meta, norm_mix_g, w_in, w_branch, w_out, hg_lb, da_lambda, da_subln_g, norm_ffn_g,
           w_ffn_in, ffn_conv_w, ffn_conv_b, w_ffn_out, norm_final_g):
    b, s, d = x.shape
    l = PAD + s
    h = jnp.concatenate([jnp.zeros((b, PAD - N_META, d), x.dtype),
                         jnp.broadcast_to(meta[None].astype(x.dtype), (b, N_META, d)), x], axis=1)
    pos = jnp.arange(l) - FIRST_VALID
    lb_soft = jax.nn.softmax(hg_lb.astype(F32), axis=0)
    lbs = jnp.cumsum(lb_soft, axis=0) - lb_soft[0]
    ret_cos, ret_sin = _rope_tables(pos, RET_DK, 1)
    da_cos, da_sin = _rope_tables(pos, DA_DH, DA_QK_W // DA_DH)
    intra, qd, kd, cd = _retention_tables()

    h2 = h.reshape(b * l, d)
    for li in range(DEPTH):
        lambda_init = 0.8 - 0.6 * math.exp(-0.3 * li)
        proj2 = _inproj(h2, norm_mix_g[li][None], w_in[li].astype(BF16))
        proj = proj2.reshape(b, l, IN_WIDTH)
        o_ret = _retention(proj, ret_cos, ret_sin, intra, qd, kd, cd)
        o_hg = _hgrn2(proj, lbs[li].reshape(HG_HEADS, 1, HG_DK))
        kr = _da_prep(proj, da_cos, da_sin)
        o_da = _diff_attn(proj, kr, da_cos[:, :LANES], da_sin[:, :LANES], da_lambda[li].astype(F32),
                          da_subln_g[li].astype(F32)[None], lambda_init)
        h2 = _merge(o_ret.reshape(b * l, -1), o_hg.reshape(b * l, -1), o_da.reshape(b * l, -1), proj2,
                    w_branch[li].astype(BF16), w_out[li].astype(BF16), h2)
        h2 = _ffn(h2, norm_ffn_g[li][None], w_ffn_in[li].astype(BF16), ffn_conv_w[li], ffn_conv_b[li][None],
                  w_ffn_out[li].astype(BF16), norm_final_g[None], l, final=li == DEPTH - 1)
    return h2.reshape(b, l, d)[:, PAD:]
```

```python
import functools
import math

import jax
import jax.numpy as jnp
from jax import lax
from jax.experimental import pallas as pl
from jax.experimental.pallas import tpu as pltpu

D_MODEL = 1024
DEPTH = 2
N_META = 16
CHUNK = 128
PAD = CHUNK
FIRST_VALID = PAD - N_META
ROPE_THETA = 10000.0
EPS = 1e-6
LB_FLOOR = 1e-30
RET_HEADS = 4
RET_DK = 128
RET_DV = 256
HG_HEADS = 8
HG_DK = 128
HG_DV = 128
DA_HEADS = 8
DA_DH = 64
DA_DV = 2 * DA_DH
N_BRANCH = 3
D_FF = 2816
CONV_W = 3
MASK_VALUE = -1e30
LOG2E = 1.4426950408889634
HG_LEVELS = 7
HG_GROUP = 2
RET_GROUP = 2

RET_QK_W = RET_HEADS * RET_DK
RET_V_W = RET_HEADS * RET_DV
HG_K_W = HG_HEADS * HG_DK
HG_V_W = HG_HEADS * HG_DV
DA_QK_W = DA_HEADS * 2 * DA_DH
DA_V_W = DA_HEADS * DA_DV
IN_SPLITS = (RET_QK_W, RET_QK_W, RET_V_W, RET_V_W, HG_K_W, HG_K_W, HG_V_W, HG_V_W,
             DA_QK_W, DA_QK_W, DA_V_W, N_BRANCH * D_MODEL)
IN_WIDTH = sum(IN_SPLITS)
(OFF_RQ, OFF_RK, OFF_RV, OFF_RG, OFF_HQ, OFF_HF, OFF_HI, OFF_HG,
 OFF_DQ, OFF_DK, OFF_DV, OFF_MG) = [sum(IN_SPLITS[:i]) for i in range(len(IN_SPLITS))]

F32 = jnp.float32
BF16 = jnp.bfloat16
LANES = 128
VMEM_LIMIT = 56 * 1024 * 1024


def _cparams(sem):
    return pltpu.CompilerParams(dimension_semantics=sem, vmem_limit_bytes=VMEM_LIMIT)


def _dot(a, b):
    return jnp.dot(a, b, preferred_element_type=F32)


def _dot_nt(a, b):
    return lax.dot_general(a, b, (((1,), (1,)), ((), ())), preferred_element_type=F32)


def _dot_tn(a, b):
    return lax.dot_general(a, b, (((0,), (0,)), ((), ())), preferred_element_type=F32)


def _silu(x):
    return x * (1.0 / (1.0 + jnp.exp(-x)))


def _sigmoid(x):
    return 1.0 / (1.0 + jnp.exp(-x))


def _rms_rows(x):
    return x * lax.rsqrt(jnp.mean(x * x, axis=-1, keepdims=True) + EPS)


def _inproj_kernel(x_ref, g_ref, w_ref, o_ref, hn_ref):
    @pl.when(pl.program_id(1) == 0)
    def _():
        hn_ref[...] = (_rms_rows(x_ref[...]) * g_ref[...]).astype(BF16)

    o_ref[...] = _dot(hn_ref[...], w_ref[...].astype(BF16)).astype(o_ref.dtype)


def _inproj(h2, g, w, li, tm=1664, tn=1024):
    t, d = h2.shape
    n = w.shape[2]
    return pl.pallas_call(
        _inproj_kernel,
        grid=(t // tm, n // tn),
        in_specs=[pl.BlockSpec((tm, d), lambda i, j: (i, 0)),
                  pl.BlockSpec((None, 1, d), lambda i, j: (li, 0, 0)),
                  pl.BlockSpec((None, d, tn), lambda i, j: (li, 0, j))],
        out_specs=pl.BlockSpec((tm, tn), lambda i, j: (i, j)),
        out_shape=jax.ShapeDtypeStruct((t, n), BF16),
        scratch_shapes=[pltpu.VMEM((tm, d), BF16)],
        compiler_params=_cparams(("parallel", "arbitrary")),
        name="inproj",
    )(h2, g, w)


def _ret_kernel(q_ref, k_ref, v_ref, g_ref, cos_ref, sin_ref, intra_ref, qd_ref, kd_ref, cd_ref,
                o_ref, st_ref, *, tl, nb):
    lblk = pl.program_id(1)

    @pl.when(lblk == 0)
    def _():
        st_ref[...] = jnp.zeros_like(st_ref)

    row = lax.broadcasted_iota(jnp.int32, (CHUNK, RET_DK), 0)

    def chunk(bi, hh, r, cos, sin):
        rows = pl.ds(r, CHUNK)
        kcols = slice(hh * RET_DK, (hh + 1) * RET_DK)
        vcols = slice(hh * RET_DV, (hh + 1) * RET_DV)
        ci = bi * RET_GROUP + hh
        q = q_ref[bi, rows, kcols].astype(F32)
        k = k_ref[bi, rows, kcols].astype(F32)
        qr = q * cos + pltpu.roll(q, RET_DK // 2, 1) * sin
        kr = (k * cos + pltpu.roll(k, RET_DK // 2, 1) * sin) * RET_DK ** -0.5
        kr = jnp.where(lblk * tl + r + row >= FIRST_VALID, kr, 0.0)
        v = v_ref[bi, rows, vcols].astype(BF16)
        state = st_ref[ci]
        s = _dot_nt(qr.astype(BF16), kr.astype(BF16)) * intra_ref[hh]
        o = _dot(s.astype(BF16), v) + _dot((qr * qd_ref[hh]).astype(BF16), state.astype(BF16))
        st_ref[ci] = cd_ref[hh] * state + _dot((kr * kd_ref[hh]).T.astype(BF16), v)
        o_ref[bi, rows, vcols] = (_rms_rows(o) * _silu(g_ref[bi, rows, vcols].astype(F32))).astype(o_ref.dtype)

    def body(c, carry):
        r = pl.multiple_of(c * CHUNK, CHUNK)
        cos = cos_ref[pl.ds(r, CHUNK), :]
        sin = sin_ref[pl.ds(r, CHUNK), :]
        for bi in range(nb):
            for hh in range(RET_GROUP):
                chunk(bi, hh, r, cos, sin)
        return carry

    lax.fori_loop(0, tl // CHUNK, body, 0)


def _retention(proj, cos, sin, intra, qd, kd, cd, tl=1664):
    b, l, _ = proj.shape
    kw = RET_GROUP * RET_DK
    vw = RET_GROUP * RET_DV

    def col(off, w):
        return lambda hi, li: (0, li, off // w + hi)

    def per_head(shape):
        return pl.BlockSpec((RET_GROUP,) + shape, lambda hi, li: (hi, 0, 0))

    return pl.pallas_call(
        functools.partial(_ret_kernel, tl=tl, nb=b),
        grid=(RET_HEADS // RET_GROUP, l // tl),
        in_specs=[pl.BlockSpec((b, tl, kw), col(OFF_RQ, kw)),
                  pl.BlockSpec((b, tl, kw), col(OFF_RK, kw)),
                  pl.BlockSpec((b, tl, vw), col(OFF_RV, vw)),
                  pl.BlockSpec((b, tl, vw), col(OFF_RG, vw)),
                  pl.BlockSpec((tl, RET_DK), lambda hi, li: (li, 0)),
                  pl.BlockSpec((tl, RET_DK), lambda hi, li: (li, 0)),
                  per_head((CHUNK, CHUNK)), per_head((CHUNK, RET_DK)), per_head((CHUNK, RET_DK)),
                  per_head((1, RET_DV))],
        out_specs=pl.BlockSpec((b, tl, vw), lambda hi, li: (0, li, hi)),
        out_shape=jax.ShapeDtypeStruct((b, l, RET_V_W), BF16),
        scratch_shapes=[pltpu.VMEM((b * RET_GROUP, RET_DK, RET_DV), F32)],
        compiler_params=_cparams(("parallel", "arbitrary")),
        name="retention",
    )(proj, proj, proj, proj, cos, sin, intra, qd, kd, cd)


def _hg_kernel(q_ref, f_ref, i_ref, g_ref, lb_ref, o_ref, st_ref, cb_sc, *, tl, nb):
    lblk = pl.program_id(1)

    @pl.when(lblk == 0)
    def _():
        st_ref[...] = jnp.zeros_like(st_ref)

    row = lax.broadcasted_iota(jnp.int32, (CHUNK, HG_DK), 0)
    lane = lax.broadcasted_iota(jnp.int32, (CHUNK, HG_DK), 1)
    tri = (row >= lane).astype(BF16)
    diff_bits = row ^ lane
    level = jnp.where(row == lane, -1, HG_LEVELS)
    for lv in range(HG_LEVELS):
        level = jnp.where((row > lane) & ((diff_bits >> lv) == 1), lv, level)
    half_row = lax.broadcasted_iota(jnp.int32, (8, HG_DK), 0) < 4
    odd_row = (row & 1) == 1

    def boundary_decay(cb, cb_ref, lv):
        s = 1 << lv
        if s == 1:
            ref_pt = jnp.where(odd_row, pltpu.roll(cb, 1, 0), cb)
        elif s == 2:
            ref_pt = jnp.concatenate(
                [jnp.where(half_row, jnp.broadcast_to(cb_ref[8 * v + 1:8 * v + 2, :], (8, HG_DK)),
                           jnp.broadcast_to(cb_ref[8 * v + 5:8 * v + 6, :], (8, HG_DK)))
                 for v in range(CHUNK // 8)], axis=0)
        else:
            ref_pt = jnp.concatenate(
                [jnp.broadcast_to(cb_ref[p * 2 * s + s - 1:p * 2 * s + s, :], (2 * s, HG_DK))
                 for p in range(CHUNK // (2 * s))], axis=0)
        return jnp.exp(-jnp.abs(cb - ref_pt))

    def chunk(bi, hh, r):
        rows = pl.ds(r, CHUNK)
        cols = slice(hh * HG_DK, (hh + 1) * HG_DK)
        ci = bi * HG_GROUP + hh
        cb_ref = cb_sc.at[ci]
        lb = lb_ref[hh]
        log_lb = jnp.log(jnp.maximum(lb, LB_FLOOR))
        log1m_lb = jnp.log1p(-lb)
        one_m_lb = 1.0 - lb
        z = f_ref[bi, rows, cols].astype(F32)
        q = q_ref[bi, rows, cols].astype(F32)
        e = jnp.exp(-jnp.abs(z))
        den = 1.0 + e
        bb = log1m_lb + (jnp.minimum(z, 0.0) - jnp.log(den))
        log_f = jnp.maximum(log_lb, bb) + jnp.log(1.0 + jnp.exp(-jnp.abs(log_lb - bb)))
        k = one_m_lb * (jnp.where(z > 0.0, e, 1.0) / den)
        v = jnp.where(lblk * tl + r + row >= FIRST_VALID, i_ref[bi, rows, cols].astype(F32), 0.0).astype(BF16)
        f_hi = log_f.astype(BF16)
        rest = log_f - f_hi.astype(F32)
        f_mid = rest.astype(BF16)
        f_lo = (rest - f_mid.astype(F32)).astype(BF16)
        cb = _dot(tri, f_hi) + _dot(tri, f_mid) + _dot(tri, f_lo)
        cb_ref[...] = cb
        state = st_ref[ci]
        cend = cb_ref[CHUNK - 1:CHUNK, :]
        o = _dot((q * jnp.exp(cb)).astype(BF16), state.astype(BF16))
        kdec = k * jnp.exp(cend - cb)
        dec_col = jnp.broadcast_to(jnp.exp(cend), (HG_DK, HG_DK)).T
        st_ref[ci] = dec_col * state + _dot(kdec.T.astype(BF16), v)

        amat = jnp.where(level == -1, _dot_nt(q.astype(BF16), k.astype(BF16)), 0.0)
        for lv in range(HG_LEVELS):
            w = boundary_decay(cb, cb_ref, lv)
            amat = jnp.where(level == lv, _dot_nt((q * w).astype(BF16), (k * w).astype(BF16)), amat)
        o = o + _dot(amat.astype(BF16), v)
        o_ref[bi, rows, cols] = (_rms_rows(o) * _silu(g_ref[bi, rows, cols].astype(F32))).astype(o_ref.dtype)

    def body(c, carry):
        r = pl.multiple_of(c * CHUNK, CHUNK)
        for bi in range(nb):
            for hh in range(HG_GROUP):
                chunk(bi, hh, r)
        return carry

    lax.fori_loop(0, tl // CHUNK, body, 0)


def _hgrn2(proj, lb, tl=1664):
    b, l, _ = proj.shape
    gw = HG_GROUP * HG_DK

    def col(off):
        return lambda hi, li: (0, li, off // gw + hi)

    return pl.pallas_call(
        functools.partial(_hg_kernel, tl=tl, nb=b),
        grid=(HG_HEADS // HG_GROUP, l // tl),
        in_specs=[pl.BlockSpec((b, tl, gw), col(OFF_HQ)),
                  pl.BlockSpec((b, tl, gw), col(OFF_HF)),
                  pl.BlockSpec((b, tl, gw), col(OFF_HI)),
                  pl.BlockSpec((b, tl, gw), col(OFF_HG)),
                  pl.BlockSpec((HG_GROUP, 1, HG_DK), lambda hi, li: (hi, 0, 0))],
        out_specs=pl.BlockSpec((b, tl, gw), lambda hi, li: (0, li, hi)),
        out_shape=jax.ShapeDtypeStruct((b, l, HG_V_W), BF16),
        scratch_shapes=[pltpu.VMEM((b * HG_GROUP, HG_DK, HG_DV), F32),
                        pltpu.VMEM((b * HG_GROUP, CHUNK, HG_DK), F32)],
        compiler_params=_cparams(("parallel", "arbitrary")),
        name="hgrn2",
    )(proj, proj, proj, proj, lb)


def _da_rope(x, cos, sin):
    lane = lax.broadcasted_iota(jnp.int32, x.shape, 1)
    first_half = (lane % DA_DH) < DA_DH // 2
    swapped = jnp.where(first_half, pltpu.roll(x, LANES - DA_DH // 2, 1), pltpu.roll(x, DA_DH // 2, 1))
    return x * cos + swapped * sin


def _da_prep_kernel(k_ref, cos_ref, sin_ref, ko_ref):
    cos = cos_ref[...]
    sin = sin_ref[...]
    for h in range(DA_HEADS):
        cols = slice(h * LANES, (h + 1) * LANES)
        ko_ref[:, cols] = _da_rope(k_ref[:, cols].astype(F32), cos, sin).astype(BF16)


def _da_prep(proj, cos, sin, tl=1664):
    b, l, _ = proj.shape
    return pl.pallas_call(
        _da_prep_kernel,
        grid=(b, l // tl),
        in_specs=[pl.BlockSpec((None, tl, DA_QK_W), lambda bi, li: (bi, li, OFF_DK // DA_QK_W)),
                  pl.BlockSpec((tl, LANES), lambda bi, li: (li, 0)),
                  pl.BlockSpec((tl, LANES), lambda bi, li: (li, 0))],
        out_specs=pl.BlockSpec((None, tl, DA_QK_W), lambda bi, li: (bi, li, 0)),
        out_shape=jax.ShapeDtypeStruct((b, l, DA_QK_W), BF16),
        compiler_params=_cparams(("parallel", "parallel")),
        name="da_prep",
    )(proj, cos, sin)


def _da_kernel(q_ref, cos_ref, sin_ref, k_ref, v_ref, lam_ref, sg_ref, o_ref, m_ref, l_ref, acc_ref,
               sa_ref, ba_ref, sb_ref, bb_ref, *, tq, lambda_init):
    qi = pl.program_id(2)
    q = _da_rope(q_ref[...].astype(F32), cos_ref[...], sin_ref[...]) * (DA_DH ** -0.5 * LOG2E)
    qt = q.T.astype(BF16)
    row = lax.broadcasted_iota(jnp.int32, qt.shape, 0)
    zero = jnp.zeros_like(qt)
    qmaps = (jnp.where(row < DA_DH, qt, zero), jnp.where(row >= DA_DH, qt, zero))
    m_ref[...] = jnp.full_like(m_ref, MASK_VALUE)
    l_ref[...] = jnp.zeros_like(l_ref)
    acc_ref[...] = jnp.zeros_like(acc_ref)
    kloc = lax.broadcasted_iota(jnp.int32, (tq, tq), 0)
    qpos = qi * tq + lax.broadcasted_iota(jnp.int32, (tq, tq), 1)

    bufs = ((sa_ref, ba_ref), (sb_ref, bb_ref))

    def block_rows(kb):
        return pl.ds(pl.multiple_of(kb * tq, tq), tq)

    def scores(kb, allowed, buf):
        s_ref, bmax_ref = buf
        k = k_ref[block_rows(kb), :]
        for m in range(2):
            s = _dot(k, qmaps[m])
            if allowed is not None:
                s = jnp.where(allowed, s, MASK_VALUE)
            s_ref[m] = s
            bmax_ref[m] = jnp.max(s, axis=0, keepdims=True)

    def accumulate(kb, buf):
        s_ref, bmax_ref = buf
        v = v_ref[block_rows(kb), :]
        for m in range(2):
            m_prev = m_ref[m]
            m_new = jnp.maximum(m_prev, bmax_ref[m])
            p = jnp.exp2(s_ref[m] - m_new)
            alpha = jnp.exp2(m_prev - m_new)
            l_ref[m] = alpha * l_ref[m] + jnp.sum(p, axis=0, keepdims=True)
            acc_ref[m] = alpha * acc_ref[m] + _dot_tn(v, p.astype(BF16))
            m_ref[m] = m_new

    scores(0, (kloc <= qpos) & (kloc >= FIRST_VALID), bufs[0])

    n_mid = jnp.maximum(qi - 1, 0)

    def pair(t, carry):
        j = 2 * t
        scores(j + 1, None, bufs[1])
        accumulate(j, bufs[0])
        scores(j + 2, None, bufs[0])
        accumulate(j + 1, bufs[1])
        return carry

    lax.fori_loop(0, n_mid // 2, pair, 0)
    odd = lax.rem(n_mid, 2) == 1

    @pl.when(odd)
    def _():
        scores(n_mid, None, bufs[1])
        accumulate(n_mid - 1, bufs[0])

    causal = kloc + qi * tq <= qpos
    for parity in (0, 1):
        @pl.when((qi > 0) & (odd == (parity == 1)))
        def _():
            scores(qi, causal, bufs[1 - parity])
            accumulate(qi - 1, bufs[parity])
            accumulate(qi, bufs[1 - parity])

    @pl.when(qi == 0)
    def _():
        accumulate(0, bufs[0])

    lp = lam_ref[...]
    lam = (jnp.exp(jnp.sum(lp[0:1] * lp[1:2], axis=1, keepdims=True))
           - jnp.exp(jnp.sum(lp[2:3] * lp[3:4], axis=1, keepdims=True)) + lambda_init)
    ot = acc_ref[0] / l_ref[0] - lam * (acc_ref[1] / l_ref[1])
    o_ref[...] = (_rms_rows(ot.T) * sg_ref[...] * (1.0 - lambda_init)).astype(o_ref.dtype)


def _diff_attn(proj, kr, cos, sin, lam_p, subln_g, lambda_init, tq=640):
    b, l, _ = kr.shape
    return pl.pallas_call(
        functools.partial(_da_kernel, tq=tq, lambda_init=lambda_init),
        grid=(b, DA_HEADS, l // tq),
        in_specs=[pl.BlockSpec((None, tq, LANES), lambda bi, hi, qi: (bi, qi, OFF_DQ // LANES + hi)),
                  pl.BlockSpec((tq, LANES), lambda bi, hi, qi: (qi, 0)),
                  pl.BlockSpec((tq, LANES), lambda bi, hi, qi: (qi, 0)),
                  pl.BlockSpec((None, l, LANES), lambda bi, hi, qi: (bi, 0, hi)),
                  pl.BlockSpec((None, l, DA_DV), lambda bi, hi, qi: (bi, 0, OFF_DV // DA_DV + hi)),
                  pl.BlockSpec((4, DA_DH), lambda bi, hi, qi: (0, 0)),
                  pl.BlockSpec((1, DA_DV), lambda bi, hi, qi: (0, 0))],
        out_specs=pl.BlockSpec((None, tq, DA_DV), lambda bi, hi, qi: (bi, qi, hi)),
        out_shape=jax.ShapeDtypeStruct((b, l, DA_V_W), BF16),
        scratch_shapes=[pltpu.VMEM((2, 1, tq), F32),
                        pltpu.VMEM((2, 1, tq), F32),
                        pltpu.VMEM((2, DA_DV, tq), F32),
                        pltpu.VMEM((2, tq, tq), F32), pltpu.VMEM((2, 1, tq), F32),
                        pltpu.VMEM((2, tq, tq), F32), pltpu.VMEM((2, 1, tq), F32)],
        compiler_params=_cparams(("parallel", "parallel", "arbitrary")),
        name="diff_attn",
    )(proj, cos, sin, kr, proj, lam_p, subln_g)


def _merge_kernel(o0_ref, o1_ref, o2_ref, g0_ref, g1_ref, g2_ref, wb_ref, wo_ref, h_ref, out_ref):
    y = None
    for n, (o_ref, g_ref) in enumerate(((o0_ref, g0_ref), (o1_ref, g1_ref), (o2_ref, g2_ref))):
        term = _sigmoid(g_ref[...].astype(F32)) * _dot(o_ref[...], wb_ref[n])
        y = term if y is None else y + term
    out_ref[...] = h_ref[...] + _dot(y.astype(BF16), wo_ref[...])


def _merge(o_ret, o_hg, o_da, proj2, wb, wo, li, h2, tm=256):
    t, d = h2.shape
    row = lambda i: (i, 0)
    return pl.pallas_call(
        _merge_kernel,
        grid=(t // tm,),
        in_specs=[pl.BlockSpec((tm, d), row), pl.BlockSpec((tm, d), row), pl.BlockSpec((tm, d), row),
                  pl.BlockSpec((tm, d), lambda i: (i, OFF_MG // D_MODEL)),
                  pl.BlockSpec((tm, d), lambda i: (i, OFF_MG // D_MODEL + 1)),
                  pl.BlockSpec((tm, d), lambda i: (i, OFF_MG // D_MODEL + 2)),
                  pl.BlockSpec((None, N_BRANCH, d, d), lambda i: (li, 0, 0, 0)),
                  pl.BlockSpec((None, d, d), lambda i: (li, 0, 0)),
                  pl.BlockSpec((tm, d), row)],
        out_specs=pl.BlockSpec((tm, d), row),
        out_shape=jax.ShapeDtypeStruct((t, d), F32),
        compiler_params=_cparams(("parallel",)),
        name="merge",
    )(o_ret, o_hg, o_da, proj2, proj2, proj2, wb, wo, h2)


FFN_HALO = 8
FFN_FK = 256


def _ffn_kernel(xp_ref, x_ref, g_ref, win_ref, cw_ref, cb_ref, wo_ref, fg_ref, out_ref, act_ref,
                *, tm, seq, final):
    i = pl.program_id(0)
    x = jnp.concatenate([xp_ref[...], x_ref[...]], axis=0)
    rows = i * tm - FFN_HALO + lax.broadcasted_iota(jnp.int32, (tm + FFN_HALO, 1), 0)
    valid = lax.rem(rows + seq, seq) >= FIRST_VALID
    hn = jnp.where(valid, _rms_rows(x) * g_ref[...], 0.0).astype(BF16)

    def conv(off):
        cols = pl.ds(off, FFN_FK)
        u = _dot(hn, win_ref[:, cols])
        cw = cw_ref[:, cols]
        return (cb_ref[:, cols] + cw[0:1] * pltpu.roll(u, 2, 0)[FFN_HALO:]
                + cw[1:2] * pltpu.roll(u, 1, 0)[FFN_HALO:] + cw[2:3] * u[FFN_HALO:])

    for off in range(0, D_FF, FFN_FK):
        act = _silu(conv(off)) * conv(D_FF + off)
        act_ref[:, pl.ds(off, FFN_FK)] = act.astype(BF16)
    out = x_ref[...] + _dot(act_ref[...], wo_ref[...])
    if final:
        out = _rms_rows(out) * fg_ref[...]
    out_ref[...] = out


def _ffn(h2, g, w_in, conv_w, conv_b, w_out, final_g, li, seq, final, tm=256):
    t, d = h2.shape
    hb = tm // FFN_HALO
    resident = dict(pipeline_mode=pl.Buffered(1))
    return pl.pallas_call(
        functools.partial(_ffn_kernel, tm=tm, seq=seq, final=final),
        grid=(t // tm,),
        in_specs=[pl.BlockSpec((FFN_HALO, d), lambda i: (jnp.maximum(i * hb - 1, 0), 0)),
                  pl.BlockSpec((tm, d), lambda i: (i, 0)),
                  pl.BlockSpec((None, 1, d), lambda i: (li, 0, 0)),
                  pl.BlockSpec((None, d, 2 * D_FF), lambda i: (li, 0, 0), **resident),
                  pl.BlockSpec((None, CONV_W, 2 * D_FF), lambda i: (li, 0, 0)),
                  pl.BlockSpec((None, 1, 2 * D_FF), lambda i: (li, 0, 0)),
                  pl.BlockSpec((None, D_FF, d), lambda i: (li, 0, 0), **resident),
                  pl.BlockSpec((1, d), lambda i: (0, 0))],
        out_specs=pl.BlockSpec((tm, d), lambda i: (i, 0)),
        out_shape=jax.ShapeDtypeStruct((t, d), F32),
        scratch_shapes=[pltpu.VMEM((tm, D_FF), BF16)],
        compiler_params=_cparams(("parallel",)),
        name="ffn",
    )(h2, h2, g, w_in, conv_w, conv_b, w_out, final_g)


def _rope_tables(pos, d, reps):
    inv = ROPE_THETA ** (-jnp.arange(0, d, 2, dtype=F32) / d)
    ang = pos.astype(F32)[:, None] * inv[None, :]
    cos = jnp.cos(ang)
    sin = jnp.sin(ang)
    return jnp.tile(jnp.concatenate([cos, cos], axis=1), (1, reps)), jnp.tile(jnp.concatenate([-sin, sin], axis=1), (1, reps))


def _retention_tables():
    log_g = jnp.log1p(-jnp.exp2(-5.0 - jnp.arange(RET_HEADS, dtype=F32)))
    idx = jnp.arange(CHUNK, dtype=F32)
    gap = idx[:, None] - idx[None, :]
    intra = jnp.where(gap >= 0, jnp.exp(log_g[:, None, None] * jnp.maximum(gap, 0.0)), 0.0)
    q_dec = jnp.exp(log_g[:, None] * (idx[None, :] + 1.0))
    k_dec = jnp.exp(log_g[:, None] * (CHUNK - 1.0 - idx[None, :]))
    c_dec = jnp.exp(log_g * CHUNK)
    qd = jnp.broadcast_to(q_dec[:, :, None], (RET_HEADS, CHUNK, RET_DK))
    kd = jnp.broadcast_to(k_dec[:, :, None], (RET_HEADS, CHUNK, RET_DK))
    cd = jnp.broadcast_to(c_dec[:, None, None], (RET_HEADS, 1, RET_DV))
    return intra, qd, kd, cd


def kernel(x, meta, norm_mix_g, w_in, w_branch, w_out, hg_lb, da_lambda, da_subln_g, norm_ffn_g,
           w_ffn_in, ffn_conv_w, ffn_conv_b, w_ffn_out, norm_final_g):
    b, s, d = x.shape
    l = PAD + s
    h = jnp.concatenate([jnp.zeros((b, PAD - N_META, d), x.dtype),
                         jnp.broadcast_to(meta[None].astype(x.dtype), (b, N_META, d)), x], axis=1)
    pos = jnp.arange(l) - FIRST_VALID
    lb_soft = jax.nn.softmax(hg_lb.astype(F32), axis=0)
    lbs = jnp.cumsum(lb_soft, axis=0) - lb_soft[0]
    ret_cos, ret_sin = _rope_tables(pos, RET_DK, 1)
    da_cos, da_sin = _rope_tables(pos, DA_DH, LANES // DA_DH)
    intra, qd, kd, cd = _retention_tables()

    h2 = h.reshape(b * l, d)
    mix_g = norm_mix_g[:, None, :]
    ffn_g = norm_ffn_g[:, None, :]
    wb_bf, wo_bf = w_branch.astype(BF16), w_out.astype(BF16)
    wfi_bf, wfo_bf = w_ffn_in.astype(BF16), w_ffn_out.astype(BF16)
    conv_b = ffn_conv_b[:, None, :]
    for li in range(DEPTH):
        lambda_init = 0.8 - 0.6 * math.exp(-0.3 * li)
        proj2 = _inproj(h2, mix_g, w_in, li)
        proj = proj2.reshape(b, l, IN_WIDTH)
        o_ret = _retention(proj, ret_cos, ret_sin, intra, qd, kd, cd)
        o_hg = _hgrn2(proj, lbs[li].reshape(HG_HEADS, 1, HG_DK))
        kr = _da_prep(proj, da_cos, da_sin)
        o_da = _diff_attn(proj, kr, da_cos, da_sin, da_lambda[li].astype(F32),
                          da_subln_g[li].astype(F32)[None], lambda_init)
        h2 = _merge(o_ret.reshape(b * l, -1), o_hg.reshape(b * l, -1), o_da.reshape(b * l, -1), proj2,
                    wb_bf, wo_bf, li, h2)
        h2 = _ffn(h2, ffn_g, wfi_bf, ffn_conv_w, conv_b, wfo_bf, norm_final_g[None], li, l,
                  final=li == DEPTH - 1)
    return h2.reshape(b, l, d)[:, PAD:]
```

```python
import functools
import math

import jax
import jax.numpy as jnp
from jax import lax
from jax.experimental import pallas as pl
from jax.experimental.pallas import tpu as pltpu

D_MODEL = 1024
DEPTH = 2
N_META = 16
CHUNK = 128
PAD = CHUNK
FIRST_VALID = PAD - N_META
ROPE_THETA = 10000.0
EPS = 1e-6
LB_FLOOR = 1e-30
RET_HEADS = 4
RET_DK = 128
RET_DV = 256
HG_HEADS = 8
HG_DK = 128
HG_DV = 128
DA_HEADS = 8
DA_DH = 64
DA_DV = 2 * DA_DH
N_BRANCH = 3
D_FF = 2816
CONV_W = 3
MASK_VALUE = -1e30
LOG2E = 1.4426950408889634
HG_LEVELS = 7
HG_GROUP = 2
RET_GROUP = 2

RET_QK_W = RET_HEADS * RET_DK
RET_V_W = RET_HEADS * RET_DV
HG_K_W = HG_HEADS * HG_DK
HG_V_W = HG_HEADS * HG_DV
DA_QK_W = DA_HEADS * 2 * DA_DH
DA_V_W = DA_HEADS * DA_DV
IN_SPLITS = (RET_QK_W, RET_QK_W, RET_V_W, RET_V_W, HG_K_W, HG_K_W, HG_V_W, HG_V_W,
             DA_QK_W, DA_QK_W, DA_V_W, N_BRANCH * D_MODEL)
IN_WIDTH = sum(IN_SPLITS)
(OFF_RQ, OFF_RK, OFF_RV, OFF_RG, OFF_HQ, OFF_HF, OFF_HI, OFF_HG,
 OFF_DQ, OFF_DK, OFF_DV, OFF_MG) = [sum(IN_SPLITS[:i]) for i in range(len(IN_SPLITS))]

F32 = jnp.float32
BF16 = jnp.bfloat16
LANES = 128
VMEM_LIMIT = 56 * 1024 * 1024


def _cparams(sem):
    return pltpu.CompilerParams(dimension_semantics=sem, vmem_limit_bytes=VMEM_LIMIT)


def _dot(a, b):
    return jnp.dot(a, b, preferred_element_type=F32)


def _dot_nt(a, b):
    return lax.dot_general(a, b, (((1,), (1,)), ((), ())), preferred_element_type=F32)


def _dot_tn(a, b):
    return lax.dot_general(a, b, (((0,), (0,)), ((), ())), preferred_element_type=F32)


def _silu(x):
    return x * (1.0 / (1.0 + jnp.exp(-x)))


def _sigmoid(x):
    return 1.0 / (1.0 + jnp.exp(-x))


def _rms_rows(x):
    return x * lax.rsqrt(jnp.mean(x * x, axis=-1, keepdims=True) + EPS)


def _inproj_kernel(x_ref, g_ref, w_ref, o_ref, hn_ref):
    @pl.when(pl.program_id(1) == 0)
    def _():
        hn_ref[...] = (_rms_rows(x_ref[...]) * g_ref[...]).astype(BF16)

    o_ref[...] = _dot(hn_ref[...], w_ref[...].astype(BF16)).astype(o_ref.dtype)


def _inproj(h2, g, w, li, tm=1664, tn=1024):
    t, d = h2.shape
    n = w.shape[2]
    return pl.pallas_call(
        _inproj_kernel,
        grid=(t // tm, n // tn),
        in_specs=[pl.BlockSpec((tm, d), lambda i, j: (i, 0)),
                  pl.BlockSpec((None, 1, d), lambda i, j: (li, 0, 0)),
                  pl.BlockSpec((None, d, tn), lambda i, j: (li, 0, j))],
        out_specs=pl.BlockSpec((tm, tn), lambda i, j: (i, j)),
        out_shape=jax.ShapeDtypeStruct((t, n), BF16),
        scratch_shapes=[pltpu.VMEM((tm, d), BF16)],
        compiler_params=_cparams(("parallel", "arbitrary")),
        name="inproj",
    )(h2, g, w)


def _ret_kernel(q_ref, k_ref, v_ref, g_ref, cos_ref, sin_ref, intra_ref, qd_ref, kd_ref, cd_ref,
                o_ref, st_ref, *, tl, nb):
    lblk = pl.program_id(1)

    @pl.when(lblk == 0)
    def _():
        st_ref[...] = jnp.zeros_like(st_ref)

    row = lax.broadcasted_iota(jnp.int32, (CHUNK, RET_DK), 0)

    def chunk(bi, hh, r, cos, sin):
        rows = pl.ds(r, CHUNK)
        kcols = slice(hh * RET_DK, (hh + 1) * RET_DK)
        vcols = slice(hh * RET_DV, (hh + 1) * RET_DV)
        ci = bi * RET_GROUP + hh
        q = q_ref[bi, rows, kcols].astype(F32)
        k = k_ref[bi, rows, kcols].astype(F32)
        qr = q * cos + pltpu.roll(q, RET_DK // 2, 1) * sin
        kr = (k * cos + pltpu.roll(k, RET_DK // 2, 1) * sin) * RET_DK ** -0.5
        kr = jnp.where(lblk * tl + r + row >= FIRST_VALID, kr, 0.0)
        v = v_ref[bi, rows, vcols].astype(BF16)
        state = st_ref[ci]
        s = _dot_nt(qr.astype(BF16), kr.astype(BF16)) * intra_ref[hh]
        o = _dot(s.astype(BF16), v) + _dot((qr * qd_ref[hh]).astype(BF16), state.astype(BF16))
        st_ref[ci] = cd_ref[hh] * state + _dot((kr * kd_ref[hh]).T.astype(BF16), v)
        o_ref[bi, rows, vcols] = (_rms_rows(o) * _silu(g_ref[bi, rows, vcols].astype(F32))).astype(o_ref.dtype)

    def body(c, carry):
        r = pl.multiple_of(c * CHUNK, CHUNK)
        cos = cos_ref[pl.ds(r, CHUNK), :]
        sin = sin_ref[pl.ds(r, CHUNK), :]
        for bi in range(nb):
            for hh in range(RET_GROUP):
                chunk(bi, hh, r, cos, sin)
        return carry

    lax.fori_loop(0, tl // CHUNK, body, 0)


def _retention(proj, cos, sin, intra, qd, kd, cd, tl=1664):
    b, l, _ = proj.shape
    kw = RET_GROUP * RET_DK
    vw = RET_GROUP * RET_DV

    def col(off, w):
        return lambda hi, li: (0, li, off // w + hi)

    def per_head(shape):
        return pl.BlockSpec((RET_GROUP,) + shape, lambda hi, li: (hi, 0, 0))

    return pl.pallas_call(
        functools.partial(_ret_kernel, tl=tl, nb=b),
        grid=(RET_HEADS // RET_GROUP, l // tl),
        in_specs=[pl.BlockSpec((b, tl, kw), col(OFF_RQ, kw)),
                  pl.BlockSpec((b, tl, kw), col(OFF_RK, kw)),
                  pl.BlockSpec((b, tl, vw), col(OFF_RV, vw)),
                  pl.BlockSpec((b, tl, vw), col(OFF_RG, vw)),
                  pl.BlockSpec((tl, RET_DK), lambda hi, li: (li, 0)),
                  pl.BlockSpec((tl, RET_DK), lambda hi, li: (li, 0)),
                  per_head((CHUNK, CHUNK)), per_head((CHUNK, RET_DK)), per_head((CHUNK, RET_DK)),
                  per_head((1, RET_DV))],
        out_specs=pl.BlockSpec((b, tl, vw), lambda hi, li: (0, li, hi)),
        out_shape=jax.ShapeDtypeStruct((b, l, RET_V_W), BF16),
        scratch_shapes=[pltpu.VMEM((b * RET_GROUP, RET_DK, RET_DV), F32)],
        compiler_params=_cparams(("parallel", "arbitrary")),
        name="retention",
    )(proj, proj, proj, proj, cos, sin, intra, qd, kd, cd)


def _hg_kernel(q_ref, f_ref, i_ref, g_ref, lb_ref, o_ref, st_ref, cb_sc, *, tl, nb):
    lblk = pl.program_id(1)

    @pl.when(lblk == 0)
    def _():
        st_ref[...] = jnp.zeros_like(st_ref)

    row = lax.broadcasted_iota(jnp.int32, (CHUNK, HG_DK), 0)
    lane = lax.broadcasted_iota(jnp.int32, (CHUNK, HG_DK), 1)
    tri = (row >= lane).astype(BF16)
    diff_bits = row ^ lane
    level = jnp.where(row == lane, -1, HG_LEVELS)
    for lv in range(HG_LEVELS):
        level = jnp.where((row > lane) & ((diff_bits >> lv) == 1), lv, level)
    half_row = lax.broadcasted_iota(jnp.int32, (8, HG_DK), 0) < 4
    odd_row = (row & 1) == 1

    def boundary_decay(cb, cb_ref, lv):
        s = 1 << lv
        if s == 1:
            ref_pt = jnp.where(odd_row, pltpu.roll(cb, 1, 0), cb)
        elif s == 2:
            ref_pt = jnp.concatenate(
                [jnp.where(half_row, jnp.broadcast_to(cb_ref[8 * v + 1:8 * v + 2, :], (8, HG_DK)),
                           jnp.broadcast_to(cb_ref[8 * v + 5:8 * v + 6, :], (8, HG_DK)))
                 for v in range(CHUNK // 8)], axis=0)
        else:
            ref_pt = jnp.concatenate(
                [jnp.broadcast_to(cb_ref[p * 2 * s + s - 1:p * 2 * s + s, :], (2 * s, HG_DK))
                 for p in range(CHUNK // (2 * s))], axis=0)
        return jnp.exp(-jnp.abs(cb - ref_pt))

    def chunk(bi, hh, r):
        rows = pl.ds(r, CHUNK)
        cols = slice(hh * HG_DK, (hh + 1) * HG_DK)
        ci = bi * HG_GROUP + hh
        cb_ref = cb_sc.at[ci]
        lb = lb_ref[hh]
        log_lb = jnp.log(jnp.maximum(lb, LB_FLOOR))
        log1m_lb = jnp.log1p(-lb)
        one_m_lb = 1.0 - lb
        z = f_ref[bi, rows, cols].astype(F32)
        q = q_ref[bi, rows, cols].astype(F32)
        e = jnp.exp(-jnp.abs(z))
        den = 1.0 + e
        bb = log1m_lb + (jnp.minimum(z, 0.0) - jnp.log(den))
        log_f = jnp.maximum(log_lb, bb) + jnp.log(1.0 + jnp.exp(-jnp.abs(log_lb - bb)))
        k = one_m_lb * (jnp.where(z > 0.0, e, 1.0) / den)
        v = jnp.where(lblk * tl + r + row >= FIRST_VALID, i_ref[bi, rows, cols].astype(F32), 0.0).astype(BF16)
        f_hi = log_f.astype(BF16)
        rest = log_f - f_hi.astype(F32)
        f_mid = rest.astype(BF16)
        f_lo = (rest - f_mid.astype(F32)).astype(BF16)
        cb = _dot(tri, f_hi) + _dot(tri, f_mid) + _dot(tri, f_lo)
        cb_ref[...] = cb
        state = st_ref[ci]
        cend = cb_ref[CHUNK - 1:CHUNK, :]
        o = _dot((q * jnp.exp(cb)).astype(BF16), state.astype(BF16))
        kdec = k * jnp.exp(cend - cb)
        dec_col = jnp.broadcast_to(jnp.exp(cend), (HG_DK, HG_DK)).T
        st_ref[ci] = dec_col * state + _dot(kdec.T.astype(BF16), v)

        amat = jnp.where(level == -1, _dot_nt(q.astype(BF16), k.astype(BF16)), 0.0)
        for lv in range(HG_LEVELS):
            w = boundary_decay(cb, cb_ref, lv)
            amat = jnp.where(level == lv, _dot_nt((q * w).astype(BF16), (k * w).astype(BF16)), amat)
        o = o + _dot(amat.astype(BF16), v)
        o_ref[bi, rows, cols] = (_rms_rows(o) * _silu(g_ref[bi, rows, cols].astype(F32))).astype(o_ref.dtype)

    def body(c, carry):
        r = pl.multiple_of(c * CHUNK, CHUNK)
        for bi in range(nb):
            for hh in range(HG_GROUP):
                chunk(bi, hh, r)
        return carry

    lax.fori_loop(0, tl // CHUNK, body, 0)


def _hgrn2(proj, lb, tl=1664):
    b, l, _ = proj.shape
    gw = HG_GROUP * HG_DK

    def col(off):
        return lambda hi, li: (0, li, off // gw + hi)

    return pl.pallas_call(
        functools.partial(_hg_kernel, tl=tl, nb=b),
        grid=(HG_HEADS // HG_GROUP, l // tl),
        in_specs=[pl.BlockSpec((b, tl, gw), col(OFF_HQ)),
                  pl.BlockSpec((b, tl, gw), col(OFF_HF)),
                  pl.BlockSpec((b, tl, gw), col(OFF_HI)),
                  pl.BlockSpec((b, tl, gw), col(OFF_HG)),
                  pl.BlockSpec((HG_GROUP, 1, HG_DK), lambda hi, li: (hi, 0, 0))],
        out_specs=pl.BlockSpec((b, tl, gw), lambda hi, li: (0, li, hi)),
        out_shape=jax.ShapeDtypeStruct((b, l, HG_V_W), BF16),
        scratch_shapes=[pltpu.VMEM((b * HG_GROUP, HG_DK, HG_DV), F32),
                        pltpu.VMEM((b * HG_GROUP, CHUNK, HG_DK), F32)],
        compiler_params=_cparams(("parallel", "arbitrary")),
        name="hgrn2",
    )(proj, proj, proj, proj, lb)


def _da_rope(x, cos, sin):
    lane = lax.broadcasted_iota(jnp.int32, x.shape, 1)
    first_half = (lane % DA_DH) < DA_DH // 2
    swapped = jnp.where(first_half, pltpu.roll(x, LANES - DA_DH // 2, 1), pltpu.roll(x, DA_DH // 2, 1))
    return x * cos + swapped * sin


def _da_prep_kernel(k_ref, cos_ref, sin_ref, ko_ref):
    cos = cos_ref[...]
    sin = sin_ref[...]
    for h in range(DA_HEADS):
        cols = slice(h * LANES, (h + 1) * LANES)
        ko_ref[:, cols] = _da_rope(k_ref[:, cols].astype(F32), cos, sin).astype(BF16)


def _da_prep(proj, cos, sin, tl=1664):
    b, l, _ = proj.shape
    return pl.pallas_call(
        _da_prep_kernel,
        grid=(b, l // tl),
        in_specs=[pl.BlockSpec((None, tl, DA_QK_W), lambda bi, li: (bi, li, OFF_DK // DA_QK_W)),
                  pl.BlockSpec((tl, LANES), lambda bi, li: (li, 0)),
                  pl.BlockSpec((tl, LANES), lambda bi, li: (li, 0))],
        out_specs=pl.BlockSpec((None, tl, DA_QK_W), lambda bi, li: (bi, li, 0)),
        out_shape=jax.ShapeDtypeStruct((b, l, DA_QK_W), BF16),
        compiler_params=_cparams(("parallel", "parallel")),
        name="da_prep",
    )(proj, cos, sin)


def _da_kernel(q_ref, cos_ref, sin_ref, k_ref, v_ref, lam_ref, sg_ref, o_ref, m_ref, l_ref, acc_ref, qm_ref,
               sc_ref, bc_ref, s0_ref, b0_ref, s1_ref, b1_ref, *, tq, nq, lambda_init):
    buf_c = (sc_ref, bc_ref)
    buf_b = ((s0_ref, b0_ref), (s1_ref, b1_ref))
    krow = lax.broadcasted_iota(jnp.int32, (tq, tq), 0)
    qcol = lax.broadcasted_iota(jnp.int32, (tq, tq), 1)
    causal = krow <= qcol
    valid0 = krow >= FIRST_VALID
    maprow = lax.broadcasted_iota(jnp.int32, (2 * DA_DH, tq), 0)
    lp = lam_ref[...]
    lam = (jnp.exp(jnp.sum(lp[0:1] * lp[1:2], axis=1, keepdims=True))
           - jnp.exp(jnp.sum(lp[2:3] * lp[3:4], axis=1, keepdims=True)) + lambda_init)

    def block_rows(i):
        return pl.ds(pl.multiple_of(i * tq, tq), tq)

    def load_queries(qi):
        rows = block_rows(qi)
        q = _da_rope(q_ref[rows, :].astype(F32), cos_ref[rows, :], sin_ref[rows, :]) * (DA_DH ** -0.5 * LOG2E)
        qt = q.T.astype(BF16)
        zero = jnp.zeros_like(qt)
        qm_ref[0] = jnp.where(maprow < DA_DH, qt, zero)
        qm_ref[1] = jnp.where(maprow >= DA_DH, qt, zero)

    def reset():
        m_ref[...] = jnp.full_like(m_ref, MASK_VALUE)
        l_ref[...] = jnp.zeros_like(l_ref)
        acc_ref[...] = jnp.zeros_like(acc_ref)

    def scores(kb, allowed, buf):
        s_ref, bmax_ref = buf
        k = k_ref[block_rows(kb), :]
        for m in range(2):
            s = _dot(k, qm_ref[m])
            if allowed is not None:
                s = jnp.where(allowed, s, MASK_VALUE)
            s_ref[m] = s
            bmax_ref[m] = jnp.max(s, axis=0, keepdims=True)

    def accumulate(kb, buf):
        s_ref, bmax_ref = buf
        v = v_ref[block_rows(kb), :]
        for m in range(2):
            m_prev = m_ref[m]
            m_new = jnp.maximum(m_prev, bmax_ref[m])
            p = jnp.exp2(s_ref[m] - m_new)
            alpha = jnp.exp2(m_prev - m_new)
            l_ref[m] = alpha * l_ref[m] + jnp.sum(p, axis=0, keepdims=True)
            acc_ref[m] = alpha * acc_ref[m] + _dot_tn(v, p.astype(BF16))
            m_ref[m] = m_new

    def finish(qi):
        ot = acc_ref[0] / l_ref[0] - lam * (acc_ref[1] / l_ref[1])
        o_ref[block_rows(qi), :] = (_rms_rows(ot.T) * sg_ref[...] * (1.0 - lambda_init)).astype(o_ref.dtype)
        reset()

    def last_step(qi, buf, more):
        if more:
            load_queries(qi + 1)
            scores(0, valid0, buf_c)
        accumulate(qi, buf)
        finish(qi)

    load_queries(0)
    scores(0, causal & valid0, buf_c)
    reset()

    def query_block(qi, carry):
        @pl.when(qi == 0)
        def _():
            accumulate(0, buf_c)
            finish(0)
            if nq > 1:
                load_queries(1)
                scores(0, valid0, buf_c)

        @pl.when(qi == 1)
        def _():
            scores(1, causal, buf_b[1])
            accumulate(0, buf_c)

        @pl.when(qi >= 2)
        def _():
            scores(1, None, buf_b[1])
            accumulate(0, buf_c)

        n_plain = jnp.maximum(qi - 2, 0)

        def pair(t, c):
            j = 2 * t + 1
            scores(j + 1, None, buf_b[0])
            accumulate(j, buf_b[1])
            scores(j + 2, None, buf_b[1])
            accumulate(j + 1, buf_b[0])
            return c

        lax.fori_loop(0, n_plain // 2, pair, 0)
        odd = lax.rem(qi, 2) == 1

        @pl.when((qi >= 2) & odd)
        def _():
            scores(n_plain + 1, None, buf_b[0])
            accumulate(n_plain, buf_b[1])

        for parity in (0, 1):
            @pl.when((qi >= 2) & (odd == (parity == 1)))
            def _():
                scores(qi, causal, buf_b[parity])
                accumulate(qi - 1, buf_b[1 - parity])

            @pl.when((qi >= 1) & (qi < nq - 1) & (odd == (parity == 1)))
            def _():
                last_step(qi, buf_b[parity], True)

        if nq > 1:
            @pl.when(qi == nq - 1)
            def _():
                last_step(qi, buf_b[(nq - 1) % 2], False)

        return carry

    lax.fori_loop(0, nq, query_block, 0)


def _diff_attn(proj, kr, cos, sin, lam_p, subln_g, lambda_init, tq=640):
    b, l, _ = kr.shape
    once = dict(pipeline_mode=pl.Buffered(1))
    return pl.pallas_call(
        functools.partial(_da_kernel, tq=tq, nq=l // tq, lambda_init=lambda_init),
        grid=(b, DA_HEADS),
        in_specs=[pl.BlockSpec((None, l, LANES), lambda bi, hi: (bi, 0, OFF_DQ // LANES + hi)),
                  pl.BlockSpec((l, LANES), lambda bi, hi: (0, 0), **once),
                  pl.BlockSpec((l, LANES), lambda bi, hi: (0, 0), **once),
                  pl.BlockSpec((None, l, LANES), lambda bi, hi: (bi, 0, hi)),
                  pl.BlockSpec((None, l, DA_DV), lambda bi, hi: (bi, 0, OFF_DV // DA_DV + hi)),
                  pl.BlockSpec((4, DA_DH), lambda bi, hi: (0, 0)),
                  pl.BlockSpec((1, DA_DV), lambda bi, hi: (0, 0))],
        out_specs=pl.BlockSpec((None, l, DA_DV), lambda bi, hi: (bi, 0, hi)),
        out_shape=jax.ShapeDtypeStruct((b, l, DA_V_W), BF16),
        scratch_shapes=[pltpu.VMEM((2, 1, tq), F32),
                        pltpu.VMEM((2, 1, tq), F32),
                        pltpu.VMEM((2, DA_DV, tq), F32),
                        pltpu.VMEM((2, 2 * DA_DH, tq), BF16),
                        pltpu.VMEM((2, tq, tq), F32), pltpu.VMEM((2, 1, tq), F32),
                        pltpu.VMEM((2, tq, tq), F32), pltpu.VMEM((2, 1, tq), F32),
                        pltpu.VMEM((2, tq, tq), F32), pltpu.VMEM((2, 1, tq), F32)],
        compiler_params=_cparams(("parallel", "parallel")),
        name="diff_attn",
    )(proj, cos, sin, kr, proj, lam_p, subln_g)


def _merge_kernel(o0_ref, o1_ref, o2_ref, g0_ref, g1_ref, g2_ref, wb_ref, wo_ref, h_ref, out_ref):
    y = None
    for n, (o_ref, g_ref) in enumerate(((o0_ref, g0_ref), (o1_ref, g1_ref), (o2_ref, g2_ref))):
        term = _sigmoid(g_ref[...].astype(F32)) * _dot(o_ref[...], wb_ref[n])
        y = term if y is None else y + term
    out_ref[...] = h_ref[...] + _dot(y.astype(BF16), wo_ref[...])


def _merge(o_ret, o_hg, o_da, proj2, wb, wo, li, h2, tm=640):
    t, d = h2.shape
    row = lambda i: (i, 0)
    return pl.pallas_call(
        _merge_kernel,
        grid=(t // tm,),
        in_specs=[pl.BlockSpec((tm, d), row), pl.BlockSpec((tm, d), row), pl.BlockSpec((tm, d), row),
                  pl.BlockSpec((tm, d), lambda i: (i, OFF_MG // D_MODEL)),
                  pl.BlockSpec((tm, d), lambda i: (i, OFF_MG // D_MODEL + 1)),
                  pl.BlockSpec((tm, d), lambda i: (i, OFF_MG // D_MODEL + 2)),
                  pl.BlockSpec((None, N_BRANCH, d, d), lambda i: (li, 0, 0, 0), pipeline_mode=pl.Buffered(1)),
                  pl.BlockSpec((None, d, d), lambda i: (li, 0, 0), pipeline_mode=pl.Buffered(1)),
                  pl.BlockSpec((tm, d), row)],
        out_specs=pl.BlockSpec((tm, d), row),
        out_shape=jax.ShapeDtypeStruct((t, d), F32),
        compiler_params=_cparams(("parallel",)),
        name="merge",
    )(o_ret, o_hg, o_da, proj2, proj2, proj2, wb, wo, h2)


FFN_HALO = 8
FFN_FK = 256


def _ffn_kernel(xp_ref, x_ref, g_ref, win_ref, cw_ref, cb_ref, wo_ref, fg_ref, out_ref, act_ref,
                *, tm, seq, final):
    i = pl.program_id(0)
    x = jnp.concatenate([xp_ref[...], x_ref[...]], axis=0)
    rows = i * tm - FFN_HALO + lax.broadcasted_iota(jnp.int32, (tm + FFN_HALO, 1), 0)
    valid = lax.rem(rows + seq, seq) >= FIRST_VALID
    hn = jnp.where(valid, _rms_rows(x) * g_ref[...], 0.0).astype(BF16)

    def conv(off):
        cols = pl.ds(off, FFN_FK)
        u = _dot(hn, win_ref[:, cols])
        cw = cw_ref[:, cols]
        return (cb_ref[:, cols] + cw[0:1] * pltpu.roll(u, 2, 0)[FFN_HALO:]
                + cw[1:2] * pltpu.roll(u, 1, 0)[FFN_HALO:] + cw[2:3] * u[FFN_HALO:])

    for off in range(0, D_FF, FFN_FK):
        act = _silu(conv(off)) * conv(D_FF + off)
        act_ref[:, pl.ds(off, FFN_FK)] = act.astype(BF16)
    out = x_ref[...] + _dot(act_ref[...], wo_ref[...])
    if final:
        out = _rms_rows(out) * fg_ref[...]
    out_ref[...] = out


def _ffn(h2, g, w_in, conv_w, conv_b, w_out, final_g, li, seq, final, tm=640):
    t, d = h2.shape
    hb = tm // FFN_HALO
    resident = dict(pipeline_mode=pl.Buffered(1))
    return pl.pallas_call(
        functools.partial(_ffn_kernel, tm=tm, seq=seq, final=final),
        grid=(t // tm,),
        in_specs=[pl.BlockSpec((FFN_HALO, d), lambda i: (jnp.maximum(i * hb - 1, 0), 0)),
                  pl.BlockSpec((tm, d), lambda i: (i, 0)),
                  pl.BlockSpec((None, 1, d), lambda i: (li, 0, 0)),
                  pl.BlockSpec((None, d, 2 * D_FF), lambda i: (li, 0, 0), **resident),
                  pl.BlockSpec((None, CONV_W, 2 * D_FF), lambda i: (li, 0, 0)),
                  pl.BlockSpec((None, 1, 2 * D_FF), lambda i: (li, 0, 0)),
                  pl.BlockSpec((None, D_FF, d), lambda i: (li, 0, 0), **resident),
                  pl.BlockSpec((1, d), lambda i: (0, 0))],
        out_specs=pl.BlockSpec((tm, d), lambda i: (i, 0)),
        out_shape=jax.ShapeDtypeStruct((t, d), F32),
        scratch_shapes=[pltpu.VMEM((tm, D_FF), BF16)],
        compiler_params=_cparams(("parallel",)),
        name="ffn",
    )(h2, h2, g, w_in, conv_w, conv_b, w_out, final_g)


def _rope_tables(pos, d, reps):
    inv = ROPE_THETA ** (-jnp.arange(0, d, 2, dtype=F32) / d)
    ang = pos.astype(F32)[:, None] * inv[None, :]
    cos = jnp.cos(ang)
    sin = jnp.sin(ang)
    return jnp.tile(jnp.concatenate([cos, cos], axis=1), (1, reps)), jnp.tile(jnp.concatenate([-sin, sin], axis=1), (1, reps))


def _retention_tables():
    log_g = jnp.log1p(-jnp.exp2(-5.0 - jnp.arange(RET_HEADS, dtype=F32)))
    idx = jnp.arange(CHUNK, dtype=F32)
    gap = idx[:, None] - idx[None, :]
    intra = jnp.where(gap >= 0, jnp.exp(log_g[:, None, None] * jnp.maximum(gap, 0.0)), 0.0)
    q_dec = jnp.exp(log_g[:, None] * (idx[None, :] + 1.0))
    k_dec = jnp.exp(log_g[:, None] * (CHUNK - 1.0 - idx[None, :]))
    c_dec = jnp.exp(log_g * CHUNK)
    qd = jnp.broadcast_to(q_dec[:, :, None], (RET_HEADS, CHUNK, RET_DK))
    kd = jnp.broadcast_to(k_dec[:, :, None], (RET_HEADS, CHUNK, RET_DK))
    cd = jnp.broadcast_to(c_dec[:, None, None], (RET_HEADS, 1, RET_DV))
    return intra, qd, kd, cd


def kernel(x, meta, norm_mix_g, w_in, w_branch, w_out, hg_lb, da_lambda, da_subln_g, norm_ffn_g,
           w_ffn_in, ffn_conv_w, ffn_conv_b, w_ffn_out, norm_final_g):
    b, s, d = x.shape
    l = PAD + s
    h = jnp.concatenate([jnp.zeros((b, PAD - N_META, d), x.dtype),
                         jnp.broadcast_to(meta[None].astype(x.dtype), (b, N_META, d)), x], axis=1)
    pos = jnp.arange(l) - FIRST_VALID
    lb_soft = jax.nn.softmax(hg_lb.astype(F32), axis=0)
    lbs = jnp.cumsum(lb_soft, axis=0) - lb_soft[0]
    ret_cos, ret_sin = _rope_tables(pos, RET_DK, 1)
    da_cos, da_sin = _rope_tables(pos, DA_DH, LANES // DA_DH)
    intra, qd, kd, cd = _retention_tables()

    h2 = h.reshape(b * l, d)
    mix_g = norm_mix_g[:, None, :]
    ffn_g = norm_ffn_g[:, None, :]
    wb_bf, wo_bf = w_branch.astype(BF16), w_out.astype(BF16)
    wfi_bf, wfo_bf = w_ffn_in.astype(BF16), w_ffn_out.astype(BF16)
    conv_b = ffn_conv_b[:, None, :]
    for li in range(DEPTH):
        lambda_init = 0.8 - 0.6 * math.exp(-0.3 * li)
        proj2 = _inproj(h2, mix_g, w_in, li)
        proj = proj2.reshape(b, l, IN_WIDTH)
        o_ret = _retention(proj, ret_cos, ret_sin, intra, qd, kd, cd)
        o_hg = _hgrn2(proj, lbs[li].reshape(HG_HEADS, 1, HG_DK))
        kr = _da_prep(proj, da_cos, da_sin)
        o_da = _diff_attn(proj, kr, da_cos, da_sin, da_lambda[li].astype(F32),
                          da_subln_g[li].astype(F32)[None], lambda_init)
        h2 = _merge(o_ret.reshape(b * l, -1), o_hg.reshape(b * l, -1), o_da.reshape(b * l, -1), proj2,
                    wb_bf, wo_bf, li, h2)
        h2 = _ffn(h2, ffn_g, wfi_bf, ffn_conv_w, conv_b, wfo_bf, norm_final_g[None], li, l,
                  final=li == DEPTH - 1)
    return h2.reshape(b, l, d)[:, PAD:]
```

```python
import functools
import math

import jax
import jax.numpy as jnp
from jax import lax
from jax.experimental import pallas as pl
from jax.experimental.pallas import tpu as pltpu

D_MODEL = 1024
DEPTH = 2
N_META = 16
CHUNK = 128
PAD = CHUNK
FIRST_VALID = PAD - N_META
ROPE_THETA = 10000.0
EPS = 1e-6
LB_FLOOR = 1e-30
RET_HEADS = 4
RET_DK = 128
RET_DV = 256
HG_HEADS = 8
HG_DK = 128
HG_DV = 128
DA_HEADS = 8
DA_DH = 64
DA_DV = 2 * DA_DH
N_BRANCH = 3
D_FF = 2816
CONV_W = 3
MASK_VALUE = -1e30
LOG2E = 1.4426950408889634
HG_LEVELS = 7
HG_GROUP = 4
RET_GROUP = 2

RET_QK_W = RET_HEADS * RET_DK
RET_V_W = RET_HEADS * RET_DV
HG_K_W = HG_HEADS * HG_DK
HG_V_W = HG_HEADS * HG_DV
DA_QK_W = DA_HEADS * 2 * DA_DH
DA_V_W = DA_HEADS * DA_DV
IN_SPLITS = (RET_QK_W, RET_QK_W, RET_V_W, RET_V_W, HG_K_W, HG_K_W, HG_V_W, HG_V_W,
             DA_QK_W, DA_QK_W, DA_V_W, N_BRANCH * D_MODEL)
IN_WIDTH = sum(IN_SPLITS)
(OFF_RQ, OFF_RK, OFF_RV, OFF_RG, OFF_HQ, OFF_HF, OFF_HI, OFF_HG,
 OFF_DQ, OFF_DK, OFF_DV, OFF_MG) = [sum(IN_SPLITS[:i]) for i in range(len(IN_SPLITS))]

F32 = jnp.float32
BF16 = jnp.bfloat16
LANES = 128
VMEM_LIMIT = 56 * 1024 * 1024


def _cparams(sem):
    return pltpu.CompilerParams(dimension_semantics=sem, vmem_limit_bytes=VMEM_LIMIT)


def _dot(a, b):
    return jnp.dot(a, b, preferred_element_type=F32)


def _dot_nt(a, b):
    return lax.dot_general(a, b, (((1,), (1,)), ((), ())), preferred_element_type=F32)


def _dot_tn(a, b):
    return lax.dot_general(a, b, (((0,), (0,)), ((), ())), preferred_element_type=F32)


def _silu(x):
    return x * (1.0 / (1.0 + jnp.exp(-x)))


def _sigmoid(x):
    return 1.0 / (1.0 + jnp.exp(-x))


def _rms_rows(x):
    return x * lax.rsqrt(jnp.mean(x * x, axis=-1, keepdims=True) + EPS)


def _inproj_kernel(x_ref, g_ref, w_ref, o_ref, hn_ref):
    @pl.when(pl.program_id(1) == 0)
    def _():
        hn_ref[...] = (_rms_rows(x_ref[...]) * g_ref[...]).astype(BF16)

    o_ref[...] = _dot(hn_ref[...], w_ref[...].astype(BF16)).astype(o_ref.dtype)


def _inproj(h2, g, w, li, tm=1664, tn=1024):
    t, d = h2.shape
    n = w.shape[2]
    return pl.pallas_call(
        _inproj_kernel,
        grid=(t // tm, n // tn),
        in_specs=[pl.BlockSpec((tm, d), lambda i, j: (i, 0)),
                  pl.BlockSpec((None, 1, d), lambda i, j: (li, 0, 0)),
                  pl.BlockSpec((None, d, tn), lambda i, j: (li, 0, j))],
        out_specs=pl.BlockSpec((tm, tn), lambda i, j: (i, j)),
        out_shape=jax.ShapeDtypeStruct((t, n), BF16),
        scratch_shapes=[pltpu.VMEM((tm, d), BF16)],
        compiler_params=_cparams(("parallel", "arbitrary")),
        name="inproj",
    )(h2, g, w)


def _ret_kernel(q_ref, k_ref, v_ref, g_ref, cos_ref, sin_ref, intra_ref, qd_ref, kd_ref, cd_ref,
                o_ref, st_ref, *, tl, nb):
    lblk = pl.program_id(1)

    @pl.when(lblk == 0)
    def _():
        st_ref[...] = jnp.zeros_like(st_ref)

    row = lax.broadcasted_iota(jnp.int32, (CHUNK, RET_DK), 0)

    def chunk(bi, hh, r, cos, sin):
        rows = pl.ds(r, CHUNK)
        kcols = slice(hh * RET_DK, (hh + 1) * RET_DK)
        vcols = slice(hh * RET_DV, (hh + 1) * RET_DV)
        ci = bi * RET_GROUP + hh
        q = q_ref[bi, rows, kcols].astype(F32)
        k = k_ref[bi, rows, kcols].astype(F32)
        qr = q * cos + pltpu.roll(q, RET_DK // 2, 1) * sin
        kr = (k * cos + pltpu.roll(k, RET_DK // 2, 1) * sin) * RET_DK ** -0.5
        kr = jnp.where(lblk * tl + r + row >= FIRST_VALID, kr, 0.0)
        v = v_ref[bi, rows, vcols].astype(BF16)
        state = st_ref[ci]
        s = _dot_nt(qr.astype(BF16), kr.astype(BF16)) * intra_ref[hh]
        o = _dot(s.astype(BF16), v) + _dot((qr * qd_ref[hh]).astype(BF16), state.astype(BF16))
        st_ref[ci] = cd_ref[hh] * state + _dot((kr * kd_ref[hh]).T.astype(BF16), v)
        o_ref[bi, rows, vcols] = (_rms_rows(o) * _silu(g_ref[bi, rows, vcols].astype(F32))).astype(o_ref.dtype)

    def body(c, carry):
        r = pl.multiple_of(c * CHUNK, CHUNK)
        cos = cos_ref[pl.ds(r, CHUNK), :]
        sin = sin_ref[pl.ds(r, CHUNK), :]
        for bi in range(nb):
            for hh in range(RET_GROUP):
                chunk(bi, hh, r, cos, sin)
        return carry

    lax.fori_loop(0, tl // CHUNK, body, 0)


def _retention(proj, cos, sin, intra, qd, kd, cd, tl=1664):
    b, l, _ = proj.shape
    kw = RET_GROUP * RET_DK
    vw = RET_GROUP * RET_DV

    def col(off, w):
        return lambda hi, li: (0, li, off // w + hi)

    def per_head(shape):
        return pl.BlockSpec((RET_GROUP,) + shape, lambda hi, li: (hi, 0, 0))

    return pl.pallas_call(
        functools.partial(_ret_kernel, tl=tl, nb=b),
        grid=(RET_HEADS // RET_GROUP, l // tl),
        in_specs=[pl.BlockSpec((b, tl, kw), col(OFF_RQ, kw)),
                  pl.BlockSpec((b, tl, kw), col(OFF_RK, kw)),
                  pl.BlockSpec((b, tl, vw), col(OFF_RV, vw)),
                  pl.BlockSpec((b, tl, vw), col(OFF_RG, vw)),
                  pl.BlockSpec((tl, RET_DK), lambda hi, li: (li, 0)),
                  pl.BlockSpec((tl, RET_DK), lambda hi, li: (li, 0)),
                  per_head((CHUNK, CHUNK)), per_head((CHUNK, RET_DK)), per_head((CHUNK, RET_DK)),
                  per_head((1, RET_DV))],
        out_specs=pl.BlockSpec((b, tl, vw), lambda hi, li: (0, li, hi)),
        out_shape=jax.ShapeDtypeStruct((b, l, RET_V_W), BF16),
        scratch_shapes=[pltpu.VMEM((b * RET_GROUP, RET_DK, RET_DV), F32)],
        compiler_params=_cparams(("parallel", "arbitrary")),
        name="retention",
    )(proj, proj, proj, proj, cos, sin, intra, qd, kd, cd)


def _hg_kernel(q_ref, f_ref, i_ref, g_ref, lb_ref, o_ref, st_ref, cb_sc, *, tl, nb):
    lblk = pl.program_id(1)

    @pl.when(lblk == 0)
    def _():
        st_ref[...] = jnp.zeros_like(st_ref)

    row = lax.broadcasted_iota(jnp.int32, (CHUNK, HG_DK), 0)
    lane = lax.broadcasted_iota(jnp.int32, (CHUNK, HG_DK), 1)
    tri = (row >= lane).astype(BF16)
    diff_bits = row ^ lane
    level = jnp.where(row == lane, -1, HG_LEVELS)
    for lv in range(HG_LEVELS):
        level = jnp.where((row > lane) & ((diff_bits >> lv) == 1), lv, level)
    half_row = lax.broadcasted_iota(jnp.int32, (8, HG_DK), 0) < 4
    odd_row = (row & 1) == 1

    def boundary_decay(cb, cb_ref, lv):
        s = 1 << lv
        if s == 1:
            ref_pt = jnp.where(odd_row, pltpu.roll(cb, 1, 0), cb)
        elif s == 2:
            ref_pt = jnp.concatenate(
                [jnp.where(half_row, jnp.broadcast_to(cb_ref[8 * v + 1:8 * v + 2, :], (8, HG_DK)),
                           jnp.broadcast_to(cb_ref[8 * v + 5:8 * v + 6, :], (8, HG_DK)))
                 for v in range(CHUNK // 8)], axis=0)
        else:
            ref_pt = jnp.concatenate(
                [jnp.broadcast_to(cb_ref[p * 2 * s + s - 1:p * 2 * s + s, :], (2 * s, HG_DK))
                 for p in range(CHUNK // (2 * s))], axis=0)
        return jnp.exp(-jnp.abs(cb - ref_pt))

    def chunk(bi, hh, r):
        rows = pl.ds(r, CHUNK)
        cols = slice(hh * HG_DK, (hh + 1) * HG_DK)
        ci = bi * HG_GROUP + hh
        cb_ref = cb_sc.at[ci]
        lb = lb_ref[hh]
        log_lb = jnp.log(jnp.maximum(lb, LB_FLOOR))
        log1m_lb = jnp.log1p(-lb)
        one_m_lb = 1.0 - lb
        z = f_ref[bi, rows, cols].astype(F32)
        q = q_ref[bi, rows, cols].astype(F32)
        e = jnp.exp(-jnp.abs(z))
        den = 1.0 + e
        bb = log1m_lb + (jnp.minimum(z, 0.0) - jnp.log(den))
        log_f = jnp.maximum(log_lb, bb) + jnp.log(1.0 + jnp.exp(-jnp.abs(log_lb - bb)))
        k = one_m_lb * (jnp.where(z > 0.0, e, 1.0) / den)
        v = jnp.where(lblk * tl + r + row >= FIRST_VALID, i_ref[bi, rows, cols].astype(F32), 0.0).astype(BF16)
        f_hi = log_f.astype(BF16)
        rest = log_f - f_hi.astype(F32)
        f_mid = rest.astype(BF16)
        f_lo = (rest - f_mid.astype(F32)).astype(BF16)
        cb = _dot(tri, f_hi) + _dot(tri, f_mid) + _dot(tri, f_lo)
        cb_ref[...] = cb
        state = st_ref[ci]
        cend = cb_ref[CHUNK - 1:CHUNK, :]
        o = _dot((q * jnp.exp(cb)).astype(BF16), state.astype(BF16))
        kdec = k * jnp.exp(cend - cb)
        dec_col = jnp.broadcast_to(jnp.exp(cend), (HG_DK, HG_DK)).T
        st_ref[ci] = dec_col * state + _dot(kdec.T.astype(BF16), v)

        amat = jnp.where(level == -1, _dot_nt(q.astype(BF16), k.astype(BF16)), 0.0)
        for lv in range(HG_LEVELS):
            w = boundary_decay(cb, cb_ref, lv)
            amat = jnp.where(level == lv, _dot_nt((q * w).astype(BF16), (k * w).astype(BF16)), amat)
        o = o + _dot(amat.astype(BF16), v)
        o_ref[bi, rows, cols] = (_rms_rows(o) * _silu(g_ref[bi, rows, cols].astype(F32))).astype(o_ref.dtype)

    def body(c, carry):
        r = pl.multiple_of(c * CHUNK, CHUNK)
        for bi in range(nb):
            for hh in range(HG_GROUP):
                chunk(bi, hh, r)
        return carry

    lax.fori_loop(0, tl // CHUNK, body, 0)


def _hgrn2(proj, lb, tl=1664):
    b, l, _ = proj.shape
    gw = HG_GROUP * HG_DK

    def col(off):
        return lambda hi, li: (0, li, off // gw + hi)

    return pl.pallas_call(
        functools.partial(_hg_kernel, tl=tl, nb=b),
        grid=(HG_HEADS // HG_GROUP, l // tl),
        in_specs=[pl.BlockSpec((b, tl, gw), col(OFF_HQ)),
                  pl.BlockSpec((b, tl, gw), col(OFF_HF)),
                  pl.BlockSpec((b, tl, gw), col(OFF_HI)),
                  pl.BlockSpec((b, tl, gw), col(OFF_HG)),
                  pl.BlockSpec((HG_GROUP, 1, HG_DK), lambda hi, li: (hi, 0, 0))],
        out_specs=pl.BlockSpec((b, tl, gw), lambda hi, li: (0, li, hi)),
        out_shape=jax.ShapeDtypeStruct((b, l, HG_V_W), BF16),
        scratch_shapes=[pltpu.VMEM((b * HG_GROUP, HG_DK, HG_DV), F32),
                        pltpu.VMEM((b * HG_GROUP, CHUNK, HG_DK), F32)],
        compiler_params=_cparams(("parallel", "arbitrary")),
        name="hgrn2",
    )(proj, proj, proj, proj, lb)


def _da_rope(x, cos, sin):
    lane = lax.broadcasted_iota(jnp.int32, x.shape, 1)
    first_half = (lane % DA_DH) < DA_DH // 2
    swapped = jnp.where(first_half, pltpu.roll(x, LANES - DA_DH // 2, 1), pltpu.roll(x, DA_DH // 2, 1))
    return x * cos + swapped * sin


def _da_prep_kernel(k_ref, cos_ref, sin_ref, ko_ref):
    cos = cos_ref[...]
    sin = sin_ref[...]
    for h in range(DA_HEADS):
        cols = slice(h * LANES, (h + 1) * LANES)
        ko_ref[:, cols] = _da_rope(k_ref[:, cols].astype(F32), cos, sin).astype(BF16)


def _da_prep(proj, cos, sin, tl=1664):
    b, l, _ = proj.shape
    return pl.pallas_call(
        _da_prep_kernel,
        grid=(b, l // tl),
        in_specs=[pl.BlockSpec((None, tl, DA_QK_W), lambda bi, li: (bi, li, OFF_DK // DA_QK_W)),
                  pl.BlockSpec((tl, LANES), lambda bi, li: (li, 0)),
                  pl.BlockSpec((tl, LANES), lambda bi, li: (li, 0))],
        out_specs=pl.BlockSpec((None, tl, DA_QK_W), lambda bi, li: (bi, li, 0)),
        out_shape=jax.ShapeDtypeStruct((b, l, DA_QK_W), BF16),
        compiler_params=_cparams(("parallel", "parallel")),
        name="da_prep",
    )(proj, cos, sin)


def _da_kernel(q_ref, cos_ref, sin_ref, k_ref, v_ref, lam_ref, sg_ref, o_ref, m_ref, l_ref, acc_ref, qm_ref,
               sc_ref, bc_ref, s0_ref, b0_ref, s1_ref, b1_ref, *, tq, nq, lambda_init):
    buf_c = (sc_ref, bc_ref)
    buf_b = ((s0_ref, b0_ref), (s1_ref, b1_ref))
    krow = lax.broadcasted_iota(jnp.int32, (tq, tq), 0)
    qcol = lax.broadcasted_iota(jnp.int32, (tq, tq), 1)
    causal = krow <= qcol
    valid0 = krow >= FIRST_VALID
    maprow = lax.broadcasted_iota(jnp.int32, (2 * DA_DH, tq), 0)
    lp = lam_ref[...]
    lam = (jnp.exp(jnp.sum(lp[0:1] * lp[1:2], axis=1, keepdims=True))
           - jnp.exp(jnp.sum(lp[2:3] * lp[3:4], axis=1, keepdims=True)) + lambda_init)

    def block_rows(i):
        return pl.ds(pl.multiple_of(i * tq, tq), tq)

    def load_queries(qi):
        rows = block_rows(qi)
        q = _da_rope(q_ref[rows, :].astype(F32), cos_ref[rows, :], sin_ref[rows, :]) * (DA_DH ** -0.5 * LOG2E)
        qt = q.T.astype(BF16)
        zero = jnp.zeros_like(qt)
        qm_ref[0] = jnp.where(maprow < DA_DH, qt, zero)
        qm_ref[1] = jnp.where(maprow >= DA_DH, qt, zero)

    def reset():
        m_ref[...] = jnp.full_like(m_ref, MASK_VALUE)
        l_ref[...] = jnp.zeros_like(l_ref)
        acc_ref[...] = jnp.zeros_like(acc_ref)

    def scores(kb, allowed, buf):
        s_ref, bmax_ref = buf
        k = k_ref[block_rows(kb), :]
        for m in range(2):
            s = _dot(k, qm_ref[m])
            if allowed is not None:
                s = jnp.where(allowed, s, MASK_VALUE)
            s_ref[m] = s
            bmax_ref[m] = jnp.max(s, axis=0, keepdims=True)

    def accumulate(kb, buf):
        s_ref, bmax_ref = buf
        v = v_ref[block_rows(kb), :]
        for m in range(2):
            m_prev = m_ref[m]
            m_new = jnp.maximum(m_prev, bmax_ref[m])
            p = jnp.exp2(s_ref[m] - m_new)
            alpha = jnp.exp2(m_prev - m_new)
            l_ref[m] = alpha * l_ref[m] + jnp.sum(p, axis=0, keepdims=True)
            acc_ref[m] = alpha * acc_ref[m] + _dot_tn(v, p.astype(BF16))
            m_ref[m] = m_new

    def finish(qi):
        ot = acc_ref[0] / l_ref[0] - lam * (acc_ref[1] / l_ref[1])
        o_ref[block_rows(qi), :] = (_rms_rows(ot.T) * sg_ref[...] * (1.0 - lambda_init)).astype(o_ref.dtype)
        reset()

    def last_step(qi, buf, more):
        if more:
            load_queries(qi + 1)
            scores(0, valid0, buf_c)
        accumulate(qi, buf)
        finish(qi)

    load_queries(0)
    scores(0, causal & valid0, buf_c)
    reset()

    def query_block(qi, carry):
        @pl.when(qi == 0)
        def _():
            accumulate(0, buf_c)
            finish(0)
            if nq > 1:
                load_queries(1)
                scores(0, valid0, buf_c)

        @pl.when(qi == 1)
        def _():
            scores(1, causal, buf_b[1])
            accumulate(0, buf_c)

        @pl.when(qi >= 2)
        def _():
            scores(1, None, buf_b[1])
            accumulate(0, buf_c)

        n_plain = jnp.maximum(qi - 2, 0)

        def pair(t, c):
            j = 2 * t + 1
            scores(j + 1, None, buf_b[0])
            accumulate(j, buf_b[1])
            scores(j + 2, None, buf_b[1])
            accumulate(j + 1, buf_b[0])
            return c

        lax.fori_loop(0, n_plain // 2, pair, 0)
        odd = lax.rem(qi, 2) == 1

        @pl.when((qi >= 2) & odd)
        def _():
            scores(n_plain + 1, None, buf_b[0])
            accumulate(n_plain, buf_b[1])

        for parity in (0, 1):
            @pl.when((qi >= 2) & (odd == (parity == 1)))
            def _():
                scores(qi, causal, buf_b[parity])
                accumulate(qi - 1, buf_b[1 - parity])

            @pl.when((qi >= 1) & (qi < nq - 1) & (odd == (parity == 1)))
            def _():
                last_step(qi, buf_b[parity], True)

        if nq > 1:
            @pl.when(qi == nq - 1)
            def _():
                last_step(qi, buf_b[(nq - 1) % 2], False)

        return carry

    lax.fori_loop(0, nq, query_block, 0)


def _diff_attn(proj, kr, cos, sin, lam_p, subln_g, lambda_init, tq=640):
    b, l, _ = kr.shape
    once = dict(pipeline_mode=pl.Buffered(1))
    return pl.pallas_call(
        functools.partial(_da_kernel, tq=tq, nq=l // tq, lambda_init=lambda_init),
        grid=(b, DA_HEADS),
        in_specs=[pl.BlockSpec((None, l, LANES), lambda bi, hi: (bi, 0, OFF_DQ // LANES + hi)),
                  pl.BlockSpec((l, LANES), lambda bi, hi: (0, 0), **once),
                  pl.BlockSpec((l, LANES), lambda bi, hi: (0, 0), **once),
                  pl.BlockSpec((None, l, LANES), lambda bi, hi: (bi, 0, hi)),
                  pl.BlockSpec((None, l, DA_DV), lambda bi, hi: (bi, 0, OFF_DV // DA_DV + hi)),
                  pl.BlockSpec((4, DA_DH), lambda bi, hi: (0, 0)),
                  pl.BlockSpec((1, DA_DV), lambda bi, hi: (0, 0))],
        out_specs=pl.BlockSpec((None, l, DA_DV), lambda bi, hi: (bi, 0, hi)),
        out_shape=jax.ShapeDtypeStruct((b, l, DA_V_W), BF16),
        scratch_shapes=[pltpu.VMEM((2, 1, tq), F32),
                        pltpu.VMEM((2, 1, tq), F32),
                        pltpu.VMEM((2, DA_DV, tq), F32),
                        pltpu.VMEM((2, 2 * DA_DH, tq), BF16),
                        pltpu.VMEM((2, tq, tq), F32), pltpu.VMEM((2, 1, tq), F32),
                        pltpu.VMEM((2, tq, tq), F32), pltpu.VMEM((2, 1, tq), F32),
                        pltpu.VMEM((2, tq, tq), F32), pltpu.VMEM((2, 1, tq), F32)],
        compiler_params=_cparams(("parallel", "parallel")),
        name="diff_attn",
    )(proj, cos, sin, kr, proj, lam_p, subln_g)


def _merge_kernel(o0_ref, o1_ref, o2_ref, g0_ref, g1_ref, g2_ref, wb_ref, wo_ref, h_ref, out_ref):
    y = None
    for n, (o_ref, g_ref) in enumerate(((o0_ref, g0_ref), (o1_ref, g1_ref), (o2_ref, g2_ref))):
        term = _sigmoid(g_ref[...].astype(F32)) * _dot(o_ref[...], wb_ref[n])
        y = term if y is None else y + term
    out_ref[...] = h_ref[...] + _dot(y.astype(BF16), wo_ref[...])


def _merge(o_ret, o_hg, o_da, proj2, wb, wo, li, h2, tm=640):
    t, d = h2.shape
    row = lambda i: (i, 0)
    return pl.pallas_call(
        _merge_kernel,
        grid=(t // tm,),
        in_specs=[pl.BlockSpec((tm, d), row), pl.BlockSpec((tm, d), row), pl.BlockSpec((tm, d), row),
                  pl.BlockSpec((tm, d), lambda i: (i, OFF_MG // D_MODEL)),
                  pl.BlockSpec((tm, d), lambda i: (i, OFF_MG // D_MODEL + 1)),
                  pl.BlockSpec((tm, d), lambda i: (i, OFF_MG // D_MODEL + 2)),
                  pl.BlockSpec((None, N_BRANCH, d, d), lambda i: (li, 0, 0, 0), pipeline_mode=pl.Buffered(1)),
                  pl.BlockSpec((None, d, d), lambda i: (li, 0, 0), pipeline_mode=pl.Buffered(1)),
                  pl.BlockSpec((tm, d), row)],
        out_specs=pl.BlockSpec((tm, d), row),
        out_shape=jax.ShapeDtypeStruct((t, d), F32),
        compiler_params=_cparams(("parallel",)),
        name="merge",
    )(o_ret, o_hg, o_da, proj2, proj2, proj2, wb, wo, h2)


FFN_HALO = 8
FFN_FK = 256


def _ffn_kernel(xp_ref, x_ref, g_ref, win_ref, cw_ref, cb_ref, wo_ref, fg_ref, out_ref, act_ref,
                *, tm, seq, final, row0_fn):
    x = jnp.concatenate([xp_ref[...], x_ref[...]], axis=0)
    rows = row0_fn() - FFN_HALO + lax.broadcasted_iota(jnp.int32, (tm + FFN_HALO, 1), 0)
    valid = lax.rem(rows + seq, seq) >= FIRST_VALID
    hn = jnp.where(valid, _rms_rows(x) * g_ref[...], 0.0).astype(BF16)

    def conv(off):
        cols = pl.ds(off, FFN_FK)
        u = _dot(hn, win_ref[:, cols])
        cw = cw_ref[:, cols]
        return (cb_ref[:, cols] + cw[0:1] * pltpu.roll(u, 2, 0)[FFN_HALO:]
                + cw[1:2] * pltpu.roll(u, 1, 0)[FFN_HALO:] + cw[2:3] * u[FFN_HALO:])

    for off in range(0, D_FF, FFN_FK):
        act = _silu(conv(off)) * conv(D_FF + off)
        act_ref[:, pl.ds(off, FFN_FK)] = act.astype(BF16)
    out = x_ref[...] + _dot(act_ref[...], wo_ref[...])
    if final:
        out = _rms_rows(out) * fg_ref[...]
    out_ref[...] = out


def _ffn(h2, g, w_in, conv_w, conv_b, w_out, final_g, li, seq, final, tm=640, tm_final=512):
    t, d = h2.shape
    if final:
        tm = tm_final
        nt = (seq - PAD) // tm
        grid = (t // seq, nt)
        halo0 = lambda bi, i: bi * (seq // FFN_HALO) + PAD // FFN_HALO + i * (tm // FFN_HALO) - 1
        row0 = lambda bi, i: (halo0(bi, i) + 1) * FFN_HALO
        row0_fn = lambda: row0(pl.program_id(0), pl.program_id(1))
        x_spec = pl.BlockSpec((pl.Element(tm), pl.Element(d)), lambda bi, i: (row0(bi, i), 0))
        halo_spec = pl.BlockSpec((pl.Element(FFN_HALO), pl.Element(d)), lambda bi, i: (halo0(bi, i) * FFN_HALO, 0))
        out_spec = pl.BlockSpec((tm, d), lambda bi, i: (bi * nt + i, 0))
        out_rows = (t // seq) * (seq - PAD)
    else:
        hb = tm // FFN_HALO
        grid = (t // tm,)
        row0_fn = lambda: pl.program_id(0) * tm
        x_spec = pl.BlockSpec((tm, d), lambda i: (i, 0))
        halo_spec = pl.BlockSpec((FFN_HALO, d), lambda i: (jnp.maximum(i * hb - 1, 0), 0))
        out_spec = pl.BlockSpec((tm, d), lambda i: (i, 0))
        out_rows = t
    resident = dict(pipeline_mode=pl.Buffered(1))
    return pl.pallas_call(
        functools.partial(_ffn_kernel, tm=tm, seq=seq, final=final, row0_fn=row0_fn),
        grid=grid,
        in_specs=[halo_spec,
                  x_spec,
                  pl.BlockSpec((None, 1, d), lambda *_: (li, 0, 0)),
                  pl.BlockSpec((None, d, 2 * D_FF), lambda *_: (li, 0, 0), **resident),
                  pl.BlockSpec((None, CONV_W, 2 * D_FF), lambda *_: (li, 0, 0)),
                  pl.BlockSpec((None, 1, 2 * D_FF), lambda *_: (li, 0, 0)),
                  pl.BlockSpec((None, D_FF, d), lambda *_: (li, 0, 0), **resident),
                  pl.BlockSpec((1, d), lambda *_: (0, 0))],
        out_specs=out_spec,
        out_shape=jax.ShapeDtypeStruct((out_rows, d), F32),
        scratch_shapes=[pltpu.VMEM((tm, D_FF), BF16)],
        compiler_params=_cparams(("parallel",) * len(grid)),
        name="ffn",
    )(h2, h2, g, w_in, conv_w, conv_b, w_out, final_g)


def _rope_tables(pos, d, reps):
    inv = ROPE_THETA ** (-jnp.arange(0, d, 2, dtype=F32) / d)
    ang = pos.astype(F32)[:, None] * inv[None, :]
    cos = jnp.cos(ang)
    sin = jnp.sin(ang)
    return jnp.tile(jnp.concatenate([cos, cos], axis=1), (1, reps)), jnp.tile(jnp.concatenate([-sin, sin], axis=1), (1, reps))


def _retention_tables():
    log_g = jnp.log1p(-jnp.exp2(-5.0 - jnp.arange(RET_HEADS, dtype=F32)))
    idx = jnp.arange(CHUNK, dtype=F32)
    gap = idx[:, None] - idx[None, :]
    intra = jnp.where(gap >= 0, jnp.exp(log_g[:, None, None] * jnp.maximum(gap, 0.0)), 0.0)
    q_dec = jnp.exp(log_g[:, None] * (idx[None, :] + 1.0))
    k_dec = jnp.exp(log_g[:, None] * (CHUNK - 1.0 - idx[None, :]))
    c_dec = jnp.exp(log_g * CHUNK)
    qd = jnp.broadcast_to(q_dec[:, :, None], (RET_HEADS, CHUNK, RET_DK))
    kd = jnp.broadcast_to(k_dec[:, :, None], (RET_HEADS, CHUNK, RET_DK))
    cd = jnp.broadcast_to(c_dec[:, None, None], (RET_HEADS, 1, RET_DV))
    return intra, qd, kd, cd


def kernel(x, meta, norm_mix_g, w_in, w_branch, w_out, hg_lb, da_lambda, da_subln_g, norm_ffn_g,
           w_ffn_in, ffn_conv_w, ffn_conv_b, w_ffn_out, norm_final_g):
    b, s, d = x.shape
    l = PAD + s
    h = jnp.concatenate([jnp.zeros((b, PAD - N_META, d), x.dtype),
                         jnp.broadcast_to(meta[None].astype(x.dtype), (b, N_META, d)), x], axis=1)
    pos = jnp.arange(l) - FIRST_VALID
    lb_soft = jax.nn.softmax(hg_lb.astype(F32), axis=0)
    lbs = jnp.cumsum(lb_soft, axis=0) - lb_soft[0]
    ret_cos, ret_sin = _rope_tables(pos, RET_DK, 1)
    da_cos, da_sin = _rope_tables(pos, DA_DH, LANES // DA_DH)
    intra, qd, kd, cd = _retention_tables()

    h2 = h.reshape(b * l, d)
    mix_g = norm_mix_g[:, None, :]
    ffn_g = norm_ffn_g[:, None, :]
    wb_bf, wo_bf = w_branch.astype(BF16), w_out.astype(BF16)
    wfi_bf, wfo_bf = w_ffn_in.astype(BF16), w_ffn_out.astype(BF16)
    conv_b = ffn_conv_b[:, None, :]
    for li in range(DEPTH):
        lambda_init = 0.8 - 0.6 * math.exp(-0.3 * li)
        proj2 = _inproj(h2, mix_g, w_in, li)
        proj = proj2.reshape(b, l, IN_WIDTH)
        o_ret = _retention(proj, ret_cos, ret_sin, intra, qd, kd, cd)
        o_hg = _hgrn2(proj, lbs[li].reshape(HG_HEADS, 1, HG_DK))
        kr = _da_prep(proj, da_cos, da_sin)
        o_da = _diff_attn(proj, kr, da_cos, da_sin, da_lambda[li].astype(F32),
                          da_subln_g[li].astype(F32)[None], lambda_init)
        h2 = _merge(o_ret.reshape(b * l, -1), o_hg.reshape(b * l, -1), o_da.reshape(b * l, -1), proj2,
                    wb_bf, wo_bf, li, h2)
        h2 = _ffn(h2, ffn_g, wfi_bf, ffn_conv_w, conv_b, wfo_bf, norm_final_g[None], li, l,
                  final=li == DEPTH - 1)
    return h2.reshape(b, s, d)
```

```python
import functools
import math

import jax
import jax.numpy as jnp
from jax import lax
from jax.experimental import pallas as pl
from jax.experimental.pallas import tpu as pltpu

D_MODEL = 1024
DEPTH = 2
N_META = 16
CHUNK = 128
PAD = CHUNK
FIRST_VALID = PAD - N_META
ROPE_THETA = 10000.0
EPS = 1e-6
LB_FLOOR = 1e-30
RET_HEADS = 4
RET_DK = 128
RET_DV = 256
HG_HEADS = 8
HG_DK = 128
HG_DV = 128
DA_HEADS = 8
DA_DH = 64
DA_DV = 2 * DA_DH
N_BRANCH = 3
D_FF = 2816
CONV_W = 3
MASK_VALUE = -1e30
LOG2E = 1.4426950408889634
HG_LEVELS = 7
HG_GROUP = 4
RET_GROUP = 4

RET_QK_W = RET_HEADS * RET_DK
RET_V_W = RET_HEADS * RET_DV
HG_K_W = HG_HEADS * HG_DK
HG_V_W = HG_HEADS * HG_DV
DA_QK_W = DA_HEADS * 2 * DA_DH
DA_V_W = DA_HEADS * DA_DV
IN_SPLITS = (RET_QK_W, RET_QK_W, RET_V_W, RET_V_W, HG_K_W, HG_K_W, HG_V_W, HG_V_W,
             DA_QK_W, DA_QK_W, DA_V_W, N_BRANCH * D_MODEL)
IN_WIDTH = sum(IN_SPLITS)
(OFF_RQ, OFF_RK, OFF_RV, OFF_RG, OFF_HQ, OFF_HF, OFF_HI, OFF_HG,
 OFF_DQ, OFF_DK, OFF_DV, OFF_MG) = [sum(IN_SPLITS[:i]) for i in range(len(IN_SPLITS))]

F32 = jnp.float32
BF16 = jnp.bfloat16
LANES = 128
VMEM_LIMIT = 56 * 1024 * 1024


def _cparams(sem):
    return pltpu.CompilerParams(dimension_semantics=sem, vmem_limit_bytes=VMEM_LIMIT)


def _dot(a, b):
    return jnp.dot(a, b, preferred_element_type=F32)


def _dot_nt(a, b):
    return lax.dot_general(a, b, (((1,), (1,)), ((), ())), preferred_element_type=F32)


def _dot_tn(a, b):
    return lax.dot_general(a, b, (((0,), (0,)), ((), ())), preferred_element_type=F32)


def _silu(x):
    return x * (1.0 / (1.0 + jnp.exp(-x)))


def _sigmoid(x):
    return 1.0 / (1.0 + jnp.exp(-x))


def _rms_rows(x):
    return x * lax.rsqrt(jnp.mean(x * x, axis=-1, keepdims=True) + EPS)


def _inproj_kernel(x_ref, g_ref, w_ref, o_ref, hn_ref):
    @pl.when(pl.program_id(1) == 0)
    def _():
        hn_ref[...] = (_rms_rows(x_ref[...]) * g_ref[...]).astype(BF16)

    o_ref[...] = _dot(hn_ref[...], w_ref[...].astype(BF16)).astype(o_ref.dtype)


def _inproj(h2, g, w, li, tm=1664, tn=1024):
    t, d = h2.shape
    n = w.shape[2]
    return pl.pallas_call(
        _inproj_kernel,
        grid=(t // tm, n // tn),
        in_specs=[pl.BlockSpec((tm, d), lambda i, j: (i, 0)),
                  pl.BlockSpec((None, 1, d), lambda i, j: (li, 0, 0)),
                  pl.BlockSpec((None, d, tn), lambda i, j: (li, 0, j))],
        out_specs=pl.BlockSpec((tm, tn), lambda i, j: (i, j)),
        out_shape=jax.ShapeDtypeStruct((t, n), BF16),
        scratch_shapes=[pltpu.VMEM((tm, d), BF16)],
        compiler_params=_cparams(("parallel", "arbitrary")),
        name="inproj",
    )(h2, g, w)


def _ret_kernel(q_ref, k_ref, v_ref, g_ref, cos_ref, sin_ref, intra_ref, qd_ref, kd_ref, cd_ref,
                o_ref, st_ref, *, tl, nb):
    lblk = pl.program_id(1)

    @pl.when(lblk == 0)
    def _():
        st_ref[...] = jnp.zeros_like(st_ref)

    row = lax.broadcasted_iota(jnp.int32, (CHUNK, RET_DK), 0)

    def chunk(bi, hh, r, cos, sin):
        rows = pl.ds(r, CHUNK)
        kcols = slice(hh * RET_DK, (hh + 1) * RET_DK)
        vcols = slice(hh * RET_DV, (hh + 1) * RET_DV)
        ci = bi * RET_GROUP + hh
        q = q_ref[bi, rows, kcols].astype(F32)
        k = k_ref[bi, rows, kcols].astype(F32)
        qr = q * cos + pltpu.roll(q, RET_DK // 2, 1) * sin
        kr = (k * cos + pltpu.roll(k, RET_DK // 2, 1) * sin) * RET_DK ** -0.5
        kr = jnp.where(lblk * tl + r + row >= FIRST_VALID, kr, 0.0)
        v = v_ref[bi, rows, vcols].astype(BF16)
        state = st_ref[ci]
        s = _dot_nt(qr.astype(BF16), kr.astype(BF16)) * intra_ref[hh]
        o = _dot(s.astype(BF16), v) + _dot((qr * qd_ref[hh]).astype(BF16), state.astype(BF16))
        st_ref[ci] = cd_ref[hh] * state + _dot((kr * kd_ref[hh]).T.astype(BF16), v)
        o_ref[bi, rows, vcols] = (_rms_rows(o) * _silu(g_ref[bi, rows, vcols].astype(F32))).astype(o_ref.dtype)

    def body(c, carry):
        r = pl.multiple_of(c * CHUNK, CHUNK)
        cos = cos_ref[pl.ds(r, CHUNK), :]
        sin = sin_ref[pl.ds(r, CHUNK), :]
        for bi in range(nb):
            for hh in range(RET_GROUP):
                chunk(bi, hh, r, cos, sin)
        return carry

    lax.fori_loop(0, tl // CHUNK, body, 0)


def _retention(proj, cos, sin, intra, qd, kd, cd, tl=640):
    b, l, _ = proj.shape
    kw = RET_GROUP * RET_DK
    vw = RET_GROUP * RET_DV

    def col(off, w):
        return lambda hi, li: (0, li, off // w + hi)

    def per_head(shape):
        return pl.BlockSpec((RET_GROUP,) + shape, lambda hi, li: (hi, 0, 0))

    return pl.pallas_call(
        functools.partial(_ret_kernel, tl=tl, nb=b),
        grid=(RET_HEADS // RET_GROUP, l // tl),
        in_specs=[pl.BlockSpec((b, tl, kw), col(OFF_RQ, kw)),
                  pl.BlockSpec((b, tl, kw), col(OFF_RK, kw)),
                  pl.BlockSpec((b, tl, vw), col(OFF_RV, vw)),
                  pl.BlockSpec((b, tl, vw), col(OFF_RG, vw)),
                  pl.BlockSpec((tl, RET_DK), lambda hi, li: (li, 0)),
                  pl.BlockSpec((tl, RET_DK), lambda hi, li: (li, 0)),
                  per_head((CHUNK, CHUNK)), per_head((CHUNK, RET_DK)), per_head((CHUNK, RET_DK)),
                  per_head((1, RET_DV))],
        out_specs=pl.BlockSpec((b, tl, vw), lambda hi, li: (0, li, hi)),
        out_shape=jax.ShapeDtypeStruct((b, l, RET_V_W), BF16),
        scratch_shapes=[pltpu.VMEM((b * RET_GROUP, RET_DK, RET_DV), F32)],
        compiler_params=_cparams(("parallel", "arbitrary")),
        name="retention",
    )(proj, proj, proj, proj, cos, sin, intra, qd, kd, cd)


def _hg_kernel(q_ref, f_ref, i_ref, g_ref, lb_ref, o_ref, st_ref, cb_sc, *, tl, nb):
    lblk = pl.program_id(1)

    @pl.when(lblk == 0)
    def _():
        st_ref[...] = jnp.zeros_like(st_ref)

    row = lax.broadcasted_iota(jnp.int32, (CHUNK, HG_DK), 0)
    lane = lax.broadcasted_iota(jnp.int32, (CHUNK, HG_DK), 1)
    tri = (row >= lane).astype(BF16)
    diff_bits = row ^ lane
    level = jnp.where(row == lane, -1, HG_LEVELS)
    for lv in range(HG_LEVELS):
        level = jnp.where((row > lane) & ((diff_bits >> lv) == 1), lv, level)
    half_row = lax.broadcasted_iota(jnp.int32, (8, HG_DK), 0) < 4
    odd_row = (row & 1) == 1
    signed_log2e = [jnp.where((row & (1 << lv)) != 0, LOG2E, -LOG2E) for lv in range(HG_LEVELS)]

    def boundary_decay(cb, cb_ref, lv):
        s = 1 << lv
        if s == 1:
            ref_pt = jnp.where(odd_row, pltpu.roll(cb, 1, 0), cb)
        elif s == 2:
            ref_pt = jnp.concatenate(
                [jnp.where(half_row, jnp.broadcast_to(cb_ref[8 * v + 1:8 * v + 2, :], (8, HG_DK)),
                           jnp.broadcast_to(cb_ref[8 * v + 5:8 * v + 6, :], (8, HG_DK)))
                 for v in range(CHUNK // 8)], axis=0)
        else:
            ref_pt = jnp.concatenate(
                [jnp.broadcast_to(cb_ref[p * 2 * s + s - 1:p * 2 * s + s, :], (2 * s, HG_DK))
                 for p in range(CHUNK // (2 * s))], axis=0)
        return jnp.exp2((cb - ref_pt) * signed_log2e[lv])

    def chunk(bi, hh, r):
        rows = pl.ds(r, CHUNK)
        cols = slice(hh * HG_DK, (hh + 1) * HG_DK)
        ci = bi * HG_GROUP + hh
        cb_ref = cb_sc.at[ci]
        lb = lb_ref[hh]
        log_lb = jnp.log(jnp.maximum(lb, LB_FLOOR))
        log1m_lb = jnp.log1p(-lb)
        one_m_lb = 1.0 - lb
        z = f_ref[bi, rows, cols].astype(F32)
        q = q_ref[bi, rows, cols].astype(F32)
        e = jnp.exp(-jnp.abs(z))
        den = 1.0 + e
        bb = log1m_lb + (jnp.minimum(z, 0.0) - jnp.log(den))
        log_f = jnp.maximum(log_lb, bb) + jnp.log(1.0 + jnp.exp(-jnp.abs(log_lb - bb)))
        k = one_m_lb * (jnp.where(z > 0.0, e, 1.0) / den)
        v = jnp.where(lblk * tl + r + row >= FIRST_VALID, i_ref[bi, rows, cols].astype(F32), 0.0).astype(BF16)
        f_hi = log_f.astype(BF16)
        rest = log_f - f_hi.astype(F32)
        f_mid = rest.astype(BF16)
        f_lo = (rest - f_mid.astype(F32)).astype(BF16)
        cb = _dot(tri, f_hi) + _dot(tri, f_mid) + _dot(tri, f_lo)
        cb_ref[...] = cb
        state = st_ref[ci]
        cend = cb_ref[CHUNK - 1:CHUNK, :]
        o = _dot((q * jnp.exp(cb)).astype(BF16), state.astype(BF16))
        kdec = k * jnp.exp(cend - cb)
        dec_col = jnp.broadcast_to(jnp.exp(cend), (HG_DK, HG_DK)).T
        st_ref[ci] = dec_col * state + _dot(kdec.T.astype(BF16), v)

        amat = jnp.where(level == -1, _dot_nt(q.astype(BF16), k.astype(BF16)), 0.0)
        for lv in range(HG_LEVELS):
            w = boundary_decay(cb, cb_ref, lv)
            amat = jnp.where(level == lv, _dot_nt((q * w).astype(BF16), (k * w).astype(BF16)), amat)
        o = o + _dot(amat.astype(BF16), v)
        o_ref[bi, rows, cols] = (_rms_rows(o) * _silu(g_ref[bi, rows, cols].astype(F32))).astype(o_ref.dtype)

    def body(c, carry):
        r = pl.multiple_of(c * CHUNK, CHUNK)
        for bi in range(nb):
            for hh in range(HG_GROUP):
                chunk(bi, hh, r)
        return carry

    lax.fori_loop(0, tl // CHUNK, body, 0)


def _hgrn2(proj, lb, tl=1664):
    b, l, _ = proj.shape
    gw = HG_GROUP * HG_DK

    def col(off):
        return lambda hi, li: (0, li, off // gw + hi)

    return pl.pallas_call(
        functools.partial(_hg_kernel, tl=tl, nb=b),
        grid=(HG_HEADS // HG_GROUP, l // tl),
        in_specs=[pl.BlockSpec((b, tl, gw), col(OFF_HQ)),
                  pl.BlockSpec((b, tl, gw), col(OFF_HF)),
                  pl.BlockSpec((b, tl, gw), col(OFF_HI)),
                  pl.BlockSpec((b, tl, gw), col(OFF_HG)),
                  pl.BlockSpec((HG_GROUP, 1, HG_DK), lambda hi, li: (hi, 0, 0))],
        out_specs=pl.BlockSpec((b, tl, gw), lambda hi, li: (0, li, hi)),
        out_shape=jax.ShapeDtypeStruct((b, l, HG_V_W), BF16),
        scratch_shapes=[pltpu.VMEM((b * HG_GROUP, HG_DK, HG_DV), F32),
                        pltpu.VMEM((b * HG_GROUP, CHUNK, HG_DK), F32)],
        compiler_params=_cparams(("parallel", "arbitrary")),
        name="hgrn2",
    )(proj, proj, proj, proj, lb)


def _da_rope(x, cos, sin):
    lane = lax.broadcasted_iota(jnp.int32, x.shape, 1)
    first_half = (lane % DA_DH) < DA_DH // 2
    swapped = jnp.where(first_half, pltpu.roll(x, LANES - DA_DH // 2, 1), pltpu.roll(x, DA_DH // 2, 1))
    return x * cos + swapped * sin


def _da_prep_kernel(k_ref, cos_ref, sin_ref, ko_ref):
    cos = cos_ref[...]
    sin = sin_ref[...]
    for h in range(DA_HEADS):
        cols = slice(h * LANES, (h + 1) * LANES)
        ko_ref[:, cols] = _da_rope(k_ref[:, cols].astype(F32), cos, sin).astype(BF16)


def _da_prep(proj, cos, sin, tl=1664):
    b, l, _ = proj.shape
    return pl.pallas_call(
        _da_prep_kernel,
        grid=(b, l // tl),
        in_specs=[pl.BlockSpec((None, tl, DA_QK_W), lambda bi, li: (bi, li, OFF_DK // DA_QK_W)),
                  pl.BlockSpec((tl, LANES), lambda bi, li: (li, 0)),
                  pl.BlockSpec((tl, LANES), lambda bi, li: (li, 0))],
        out_specs=pl.BlockSpec((None, tl, DA_QK_W), lambda bi, li: (bi, li, 0)),
        out_shape=jax.ShapeDtypeStruct((b, l, DA_QK_W), BF16),
        compiler_params=_cparams(("parallel", "parallel")),
        name="da_prep",
    )(proj, cos, sin)


def _da_kernel(q_ref, cos_ref, sin_ref, k_ref, v_ref, lam_ref, sg_ref, o_ref, m_ref, l_ref, acc_ref, qm_ref,
               sc_ref, bc_ref, s0_ref, b0_ref, s1_ref, b1_ref, *, tq, nq, lambda_init):
    buf_c = (sc_ref, bc_ref)
    buf_b = ((s0_ref, b0_ref), (s1_ref, b1_ref))
    krow = lax.broadcasted_iota(jnp.int32, (tq, tq), 0)
    qcol = lax.broadcasted_iota(jnp.int32, (tq, tq), 1)
    causal = krow <= qcol
    valid0 = krow >= FIRST_VALID
    maprow = lax.broadcasted_iota(jnp.int32, (2 * DA_DH, tq), 0)
    lp = lam_ref[...]
    lam = (jnp.exp(jnp.sum(lp[0:1] * lp[1:2], axis=1, keepdims=True))
           - jnp.exp(jnp.sum(lp[2:3] * lp[3:4], axis=1, keepdims=True)) + lambda_init)

    def block_rows(i):
        return pl.ds(pl.multiple_of(i * tq, tq), tq)

    def load_queries(qi):
        rows = block_rows(qi)
        q = _da_rope(q_ref[rows, :].astype(F32), cos_ref[rows, :], sin_ref[rows, :]) * (DA_DH ** -0.5 * LOG2E)
        qt = q.T.astype(BF16)
        zero = jnp.zeros_like(qt)
        qm_ref[0] = jnp.where(maprow < DA_DH, qt, zero)
        qm_ref[1] = jnp.where(maprow >= DA_DH, qt, zero)

    def reset():
        m_ref[...] = jnp.full_like(m_ref, MASK_VALUE)
        l_ref[...] = jnp.zeros_like(l_ref)
        acc_ref[...] = jnp.zeros_like(acc_ref)

    def scores(kb, allowed, buf):
        s_ref, bmax_ref = buf
        k = k_ref[block_rows(kb), :]
        for m in range(2):
            s = _dot(k, qm_ref[m])
            if allowed is not None:
                s = jnp.where(allowed, s, MASK_VALUE)
            s_ref[m] = s
            bmax_ref[m] = jnp.max(s, axis=0, keepdims=True)

    def accumulate(kb, buf):
        s_ref, bmax_ref = buf
        v = v_ref[block_rows(kb), :]
        for m in range(2):
            m_prev = m_ref[m]
            m_new = jnp.maximum(m_prev, bmax_ref[m])
            p = jnp.exp2(s_ref[m] - m_new)
            alpha = jnp.exp2(m_prev - m_new)
            l_ref[m] = alpha * l_ref[m] + jnp.sum(p, axis=0, keepdims=True)
            acc_ref[m] = alpha * acc_ref[m] + _dot_tn(v, p.astype(BF16))
            m_ref[m] = m_new

    def finish(qi):
        ot = acc_ref[0] / l_ref[0] - lam * (acc_ref[1] / l_ref[1])
        o_ref[block_rows(qi), :] = (_rms_rows(ot.T) * sg_ref[...] * (1.0 - lambda_init)).astype(o_ref.dtype)
        reset()

    def last_step(qi, buf, more):
        if more:
            load_queries(qi + 1)
            scores(0, valid0, buf_c)
        accumulate(qi, buf)
        finish(qi)

    load_queries(0)
    scores(0, causal & valid0, buf_c)
    reset()

    def query_block(qi, carry):
        @pl.when(qi == 0)
        def _():
            accumulate(0, buf_c)
            finish(0)
            if nq > 1:
                load_queries(1)
                scores(0, valid0, buf_c)

        @pl.when(qi == 1)
        def _():
            scores(1, causal, buf_b[1])
            accumulate(0, buf_c)

        @pl.when(qi >= 2)
        def _():
            scores(1, None, buf_b[1])
            accumulate(0, buf_c)

        n_plain = jnp.maximum(qi - 2, 0)

        def pair(t, c):
            j = 2 * t + 1
            scores(j + 1, None, buf_b[0])
            accumulate(j, buf_b[1])
            scores(j + 2, None, buf_b[1])
            accumulate(j + 1, buf_b[0])
            return c

        lax.fori_loop(0, n_plain // 2, pair, 0)
        odd = lax.rem(qi, 2) == 1

        @pl.when((qi >= 2) & odd)
        def _():
            scores(n_plain + 1, None, buf_b[0])
            accumulate(n_plain, buf_b[1])

        for parity in (0, 1):
            @pl.when((qi >= 2) & (odd == (parity == 1)))
            def _():
                scores(qi, causal, buf_b[parity])
                accumulate(qi - 1, buf_b[1 - parity])

            @pl.when((qi >= 1) & (qi < nq - 1) & (odd == (parity == 1)))
            def _():
                last_step(qi, buf_b[parity], True)

        if nq > 1:
            @pl.when(qi == nq - 1)
            def _():
                last_step(qi, buf_b[(nq - 1) % 2], False)

        return carry

    lax.fori_loop(0, nq, query_block, 0)


def _diff_attn(proj, kr, cos, sin, lam_p, subln_g, lambda_init, tq=640):
    b, l, _ = kr.shape
    once = dict(pipeline_mode=pl.Buffered(1))
    return pl.pallas_call(
        functools.partial(_da_kernel, tq=tq, nq=l // tq, lambda_init=lambda_init),
        grid=(b, DA_HEADS),
        in_specs=[pl.BlockSpec((None, l, LANES), lambda bi, hi: (bi, 0, OFF_DQ // LANES + hi)),
                  pl.BlockSpec((l, LANES), lambda bi, hi: (0, 0), **once),
                  pl.BlockSpec((l, LANES), lambda bi, hi: (0, 0), **once),
                  pl.BlockSpec((None, l, LANES), lambda bi, hi: (bi, 0, hi)),
                  pl.BlockSpec((None, l, DA_DV), lambda bi, hi: (bi, 0, OFF_DV // DA_DV + hi)),
                  pl.BlockSpec((4, DA_DH), lambda bi, hi: (0, 0)),
                  pl.BlockSpec((1, DA_DV), lambda bi, hi: (0, 0))],
        out_specs=pl.BlockSpec((None, l, DA_DV), lambda bi, hi: (bi, 0, hi)),
        out_shape=jax.ShapeDtypeStruct((b, l, DA_V_W), BF16),
        scratch_shapes=[pltpu.VMEM((2, 1, tq), F32),
                        pltpu.VMEM((2, 1, tq), F32),
                        pltpu.VMEM((2, DA_DV, tq), F32),
                        pltpu.VMEM((2, 2 * DA_DH, tq), BF16),
                        pltpu.VMEM((2, tq, tq), F32), pltpu.VMEM((2, 1, tq), F32),
                        pltpu.VMEM((2, tq, tq), F32), pltpu.VMEM((2, 1, tq), F32),
                        pltpu.VMEM((2, tq, tq), F32), pltpu.VMEM((2, 1, tq), F32)],
        compiler_params=_cparams(("parallel", "parallel")),
        name="diff_attn",
    )(proj, cos, sin, kr, proj, lam_p, subln_g)


def _merge_kernel(o0_ref, o1_ref, o2_ref, g0_ref, g1_ref, g2_ref, wb_ref, wo_ref, h_ref, out_ref):
    y = None
    for n, (o_ref, g_ref) in enumerate(((o0_ref, g0_ref), (o1_ref, g1_ref), (o2_ref, g2_ref))):
        term = _sigmoid(g_ref[...].astype(F32)) * _dot(o_ref[...], wb_ref[n])
        y = term if y is None else y + term
    out_ref[...] = h_ref[...] + _dot(y.astype(BF16), wo_ref[...])


def _merge(o_ret, o_hg, o_da, proj2, wb, wo, li, h2, tm=640):
    t, d = h2.shape
    row = lambda i: (i, 0)
    return pl.pallas_call(
        _merge_kernel,
        grid=(t // tm,),
        in_specs=[pl.BlockSpec((tm, d), row), pl.BlockSpec((tm, d), row), pl.BlockSpec((tm, d), row),
                  pl.BlockSpec((tm, d), lambda i: (i, OFF_MG // D_MODEL)),
                  pl.BlockSpec((tm, d), lambda i: (i, OFF_MG // D_MODEL + 1)),
                  pl.BlockSpec((tm, d), lambda i: (i, OFF_MG // D_MODEL + 2)),
                  pl.BlockSpec((None, N_BRANCH, d, d), lambda i: (li, 0, 0, 0), pipeline_mode=pl.Buffered(1)),
                  pl.BlockSpec((None, d, d), lambda i: (li, 0, 0), pipeline_mode=pl.Buffered(1)),
                  pl.BlockSpec((tm, d), row)],
        out_specs=pl.BlockSpec((tm, d), row),
        out_shape=jax.ShapeDtypeStruct((t, d), F32),
        compiler_params=_cparams(("parallel",)),
        name="merge",
    )(o_ret, o_hg, o_da, proj2, proj2, proj2, wb, wo, h2)


FFN_HALO = 8
FFN_FK = 256


def _ffn_kernel(xp_ref, x_ref, g_ref, win_ref, cw_ref, cb_ref, wo_ref, fg_ref, out_ref, act_ref,
                *, tm, seq, final, row0_fn):
    x = jnp.concatenate([xp_ref[...], x_ref[...]], axis=0)
    rows = row0_fn() - FFN_HALO + lax.broadcasted_iota(jnp.int32, (tm + FFN_HALO, 1), 0)
    valid = lax.rem(rows + seq, seq) >= FIRST_VALID
    hn = jnp.where(valid, _rms_rows(x) * g_ref[...], 0.0).astype(BF16)

    def conv(off):
        cols = pl.ds(off, FFN_FK)
        u = _dot(hn, win_ref[:, cols])
        cw = cw_ref[:, cols]
        return (cb_ref[:, cols] + cw[0:1] * pltpu.roll(u, 2, 0)[FFN_HALO:]
                + cw[1:2] * pltpu.roll(u, 1, 0)[FFN_HALO:] + cw[2:3] * u[FFN_HALO:])

    for off in range(0, D_FF, FFN_FK):
        act = _silu(conv(off)) * conv(D_FF + off)
        act_ref[:, pl.ds(off, FFN_FK)] = act.astype(BF16)
    out = x_ref[...] + _dot(act_ref[...], wo_ref[...])
    if final:
        out = _rms_rows(out) * fg_ref[...]
    out_ref[...] = out


def _ffn(h2, g, w_in, conv_w, conv_b, w_out, final_g, li, seq, final, tm=640, tm_final=1024):
    t, d = h2.shape
    if final:
        tm = tm_final
        nt = (seq - PAD) // tm
        grid = (t // seq, nt)
        halo0 = lambda bi, i: bi * (seq // FFN_HALO) + PAD // FFN_HALO + i * (tm // FFN_HALO) - 1
        row0 = lambda bi, i: (halo0(bi, i) + 1) * FFN_HALO
        row0_fn = lambda: row0(pl.program_id(0), pl.program_id(1))
        x_spec = pl.BlockSpec((pl.Element(tm), pl.Element(d)), lambda bi, i: (row0(bi, i), 0))
        halo_spec = pl.BlockSpec((pl.Element(FFN_HALO), pl.Element(d)), lambda bi, i: (halo0(bi, i) * FFN_HALO, 0))
        out_spec = pl.BlockSpec((tm, d), lambda bi, i: (bi * nt + i, 0))
        out_rows = (t // seq) * (seq - PAD)
    else:
        hb = tm // FFN_HALO
        grid = (t // tm,)
        row0_fn = lambda: pl.program_id(0) * tm
        x_spec = pl.BlockSpec((tm, d), lambda i: (i, 0))
        halo_spec = pl.BlockSpec((FFN_HALO, d), lambda i: (jnp.maximum(i * hb - 1, 0), 0))
        out_spec = pl.BlockSpec((tm, d), lambda i: (i, 0))
        out_rows = t
    resident = dict(pipeline_mode=pl.Buffered(1))
    return pl.pallas_call(
        functools.partial(_ffn_kernel, tm=tm, seq=seq, final=final, row0_fn=row0_fn),
        grid=grid,
        in_specs=[halo_spec,
                  x_spec,
                  pl.BlockSpec((None, 1, d), lambda *_: (li, 0, 0)),
                  pl.BlockSpec((None, d, 2 * D_FF), lambda *_: (li, 0, 0), **resident),
                  pl.BlockSpec((None, CONV_W, 2 * D_FF), lambda *_: (li, 0, 0)),
                  pl.BlockSpec((None, 1, 2 * D_FF), lambda *_: (li, 0, 0)),
                  pl.BlockSpec((None, D_FF, d), lambda *_: (li, 0, 0), **resident),
                  pl.BlockSpec((1, d), lambda *_: (0, 0))],
        out_specs=out_spec,
        out_shape=jax.ShapeDtypeStruct((out_rows, d), F32),
        scratch_shapes=[pltpu.VMEM((tm, D_FF), BF16)],
        compiler_params=_cparams(("parallel",) * len(grid)),
        name="ffn",
    )(h2, h2, g, w_in, conv_w, conv_b, w_out, final_g)


def _rope_tables(pos, d, reps):
    inv = ROPE_THETA ** (-jnp.arange(0, d, 2, dtype=F32) / d)
    ang = pos.astype(F32)[:, None] * inv[None, :]
    cos = jnp.cos(ang)
    sin = jnp.sin(ang)
    return jnp.tile(jnp.concatenate([cos, cos], axis=1), (1, reps)), jnp.tile(jnp.concatenate([-sin, sin], axis=1), (1, reps))


def _retention_tables():
    log_g = jnp.log1p(-jnp.exp2(-5.0 - jnp.arange(RET_HEADS, dtype=F32)))
    idx = jnp.arange(CHUNK, dtype=F32)
    gap = idx[:, None] - idx[None, :]
    intra = jnp.where(gap >= 0, jnp.exp(log_g[:, None, None] * jnp.maximum(gap, 0.0)), 0.0)
    q_dec = jnp.exp(log_g[:, None] * (idx[None, :] + 1.0))
    k_dec = jnp.exp(log_g[:, None] * (CHUNK - 1.0 - idx[None, :]))
    c_dec = jnp.exp(log_g * CHUNK)
    qd = jnp.broadcast_to(q_dec[:, :, None], (RET_HEADS, CHUNK, RET_DK))
    kd = jnp.broadcast_to(k_dec[:, :, None], (RET_HEADS, CHUNK, RET_DK))
    cd = jnp.broadcast_to(c_dec[:, None, None], (RET_HEADS, 1, RET_DV))
    return intra, qd, kd, cd


def kernel(x, meta, norm_mix_g, w_in, w_branch, w_out, hg_lb, da_lambda, da_subln_g, norm_ffn_g,
           w_ffn_in, ffn_conv_w, ffn_conv_b, w_ffn_out, norm_final_g):
    b, s, d = x.shape
    l = PAD + s
    h = jnp.concatenate([jnp.zeros((b, PAD - N_META, d), x.dtype),
                         jnp.broadcast_to(meta[None].astype(x.dtype), (b, N_META, d)), x], axis=1)
    pos = jnp.arange(l) - FIRST_VALID
    lb_soft = jax.nn.softmax(hg_lb.astype(F32), axis=0)
    lbs = jnp.cumsum(lb_soft, axis=0) - lb_soft[0]
    ret_cos, ret_sin = _rope_tables(pos, RET_DK, 1)
    da_cos, da_sin = _rope_tables(pos, DA_DH, LANES // DA_DH)
    intra, qd, kd, cd = _retention_tables()

    h2 = h.reshape(b * l, d)
    mix_g = norm_mix_g[:, None, :]
    ffn_g = norm_ffn_g[:, None, :]
    wb_bf, wo_bf = w_branch.astype(BF16), w_out.astype(BF16)
    wfi_bf, wfo_bf = w_ffn_in.astype(BF16), w_ffn_out.astype(BF16)
    conv_b = ffn_conv_b[:, None, :]
    for li in range(DEPTH):
        lambda_init = 0.8 - 0.6 * math.exp(-0.3 * li)
        proj2 = _inproj(h2, mix_g, w_in, li)
        proj = proj2.reshape(b, l, IN_WIDTH)
        o_ret = _retention(proj, ret_cos, ret_sin, intra, qd, kd, cd)
        o_hg = _hgrn2(proj, lbs[li].reshape(HG_HEADS, 1, HG_DK))
        kr = _da_prep(proj, da_cos, da_sin)
        o_da = _diff_attn(proj, kr, da_cos, da_sin, da_lambda[li].astype(F32),
                          da_subln_g[li].astype(F32)[None], lambda_init)
        h2 = _merge(o_ret.reshape(b * l, -1), o_hg.reshape(b * l, -1), o_da.reshape(b * l, -1), proj2,
                    wb_bf, wo_bf, li, h2)
        h2 = _ffn(h2, ffn_g, wfi_bf, ffn_conv_w, conv_b, wfo_bf, norm_final_g[None], li, l,
                  final=li == DEPTH - 1)
    return h2.reshape(b, s, d)
```

```python
import functools
import math

import jax
import jax.numpy as jnp
from jax import lax
from jax.experimental import pallas as pl
from jax.experimental.pallas import tpu as pltpu

D_MODEL = 1024
DEPTH = 2
N_META = 16
CHUNK = 128
PAD = CHUNK
FIRST_VALID = PAD - N_META
ROPE_THETA = 10000.0
EPS = 1e-6
LB_FLOOR = 1e-30
RET_HEADS = 4
RET_DK = 128
RET_DV = 256
HG_HEADS = 8
HG_DK = 128
HG_DV = 128
DA_HEADS = 8
DA_DH = 64
DA_DV = 2 * DA_DH
N_BRANCH = 3
D_FF = 2816
CONV_W = 3
MASK_VALUE = -1e30
LOG2E = 1.4426950408889634
HG_LEVELS = 7
HG_GROUP = 8
RET_GROUP = 4

RET_QK_W = RET_HEADS * RET_DK
RET_V_W = RET_HEADS * RET_DV
HG_K_W = HG_HEADS * HG_DK
HG_V_W = HG_HEADS * HG_DV
DA_QK_W = DA_HEADS * 2 * DA_DH
DA_V_W = DA_HEADS * DA_DV
IN_SPLITS = (RET_QK_W, RET_QK_W, RET_V_W, RET_V_W, HG_K_W, HG_K_W, HG_V_W, HG_V_W,
             DA_QK_W, DA_QK_W, DA_V_W, N_BRANCH * D_MODEL)
IN_WIDTH = sum(IN_SPLITS)
(OFF_RQ, OFF_RK, OFF_RV, OFF_RG, OFF_HQ, OFF_HF, OFF_HI, OFF_HG,
 OFF_DQ, OFF_DK, OFF_DV, OFF_MG) = [sum(IN_SPLITS[:i]) for i in range(len(IN_SPLITS))]

F32 = jnp.float32
BF16 = jnp.bfloat16
LANES = 128
VMEM_LIMIT = 56 * 1024 * 1024


def _cparams(sem):
    return pltpu.CompilerParams(dimension_semantics=sem, vmem_limit_bytes=VMEM_LIMIT)


def _dot(a, b):
    return jnp.dot(a, b, preferred_element_type=F32)


def _dot_nt(a, b):
    return lax.dot_general(a, b, (((1,), (1,)), ((), ())), preferred_element_type=F32)


def _dot_tn(a, b):
    return lax.dot_general(a, b, (((0,), (0,)), ((), ())), preferred_element_type=F32)


def _silu(x):
    return x * (1.0 / (1.0 + jnp.exp(-x)))


def _sigmoid(x):
    return 1.0 / (1.0 + jnp.exp(-x))


def _rms_rows(x):
    return x * lax.rsqrt(jnp.mean(x * x, axis=-1, keepdims=True) + EPS)


def _inproj_kernel(x_ref, g_ref, w_ref, o_ref, hn_ref):
    @pl.when(pl.program_id(1) == 0)
    def _():
        hn_ref[...] = (_rms_rows(x_ref[...]) * g_ref[...]).astype(BF16)

    o_ref[...] = _dot(hn_ref[...], w_ref[...].astype(BF16)).astype(o_ref.dtype)


def _inproj(h2, g, w, li, tm=1664, tn=1024):
    t, d = h2.shape
    n = w.shape[2]
    return pl.pallas_call(
        _inproj_kernel,
        grid=(t // tm, n // tn),
        in_specs=[pl.BlockSpec((tm, d), lambda i, j: (i, 0)),
                  pl.BlockSpec((None, 1, d), lambda i, j: (li, 0, 0)),
                  pl.BlockSpec((None, d, tn), lambda i, j: (li, 0, j))],
        out_specs=pl.BlockSpec((tm, tn), lambda i, j: (i, j)),
        out_shape=jax.ShapeDtypeStruct((t, n), BF16),
        scratch_shapes=[pltpu.VMEM((tm, d), BF16)],
        compiler_params=_cparams(("parallel", "arbitrary")),
        name="inproj",
    )(h2, g, w)


def _ret_kernel(q_ref, k_ref, v_ref, g_ref, cos_ref, sin_ref, intra_ref, qd_ref, kd_ref, cd_ref,
                o_ref, st_ref, *, tl, nb):
    lblk = pl.program_id(1)

    @pl.when(lblk == 0)
    def _():
        st_ref[...] = jnp.zeros_like(st_ref)

    row = lax.broadcasted_iota(jnp.int32, (CHUNK, RET_DK), 0)

    def chunk(bi, hh, r, cos, sin):
        rows = pl.ds(r, CHUNK)
        kcols = slice(hh * RET_DK, (hh + 1) * RET_DK)
        vcols = slice(hh * RET_DV, (hh + 1) * RET_DV)
        ci = bi * RET_GROUP + hh
        q = q_ref[bi, rows, kcols].astype(F32)
        k = k_ref[bi, rows, kcols].astype(F32)
        qr = q * cos + pltpu.roll(q, RET_DK // 2, 1) * sin
        kr = (k * cos + pltpu.roll(k, RET_DK // 2, 1) * sin) * RET_DK ** -0.5
        kr = jnp.where(lblk * tl + r + row >= FIRST_VALID, kr, 0.0)
        v = v_ref[bi, rows, vcols].astype(BF16)
        state = st_ref[ci]
        s = _dot_nt(qr.astype(BF16), kr.astype(BF16)) * intra_ref[hh]
        o = _dot(s.astype(BF16), v) + _dot((qr * qd_ref[hh]).astype(BF16), state.astype(BF16))
        st_ref[ci] = cd_ref[hh] * state + _dot((kr * kd_ref[hh]).T.astype(BF16), v)
        o_ref[bi, rows, vcols] = (_rms_rows(o) * _silu(g_ref[bi, rows, vcols].astype(F32))).astype(o_ref.dtype)

    def body(c, carry):
        r = pl.multiple_of(c * CHUNK, CHUNK)
        cos = cos_ref[pl.ds(r, CHUNK), :]
        sin = sin_ref[pl.ds(r, CHUNK), :]
        for bi in range(nb):
            for hh in range(RET_GROUP):
                chunk(bi, hh, r, cos, sin)
        return carry

    lax.fori_loop(0, tl // CHUNK, body, 0)


def _retention(proj, cos, sin, intra, qd, kd, cd, tl=640):
    b, l, _ = proj.shape
    kw = RET_GROUP * RET_DK
    vw = RET_GROUP * RET_DV

    def col(off, w):
        return lambda hi, li: (0, li, off // w + hi)

    def per_head(shape):
        return pl.BlockSpec((RET_GROUP,) + shape, lambda hi, li: (hi, 0, 0))

    return pl.pallas_call(
        functools.partial(_ret_kernel, tl=tl, nb=b),
        grid=(RET_HEADS // RET_GROUP, l // tl),
        in_specs=[pl.BlockSpec((b, tl, kw), col(OFF_RQ, kw)),
                  pl.BlockSpec((b, tl, kw), col(OFF_RK, kw)),
                  pl.BlockSpec((b, tl, vw), col(OFF_RV, vw)),
                  pl.BlockSpec((b, tl, vw), col(OFF_RG, vw)),
                  pl.BlockSpec((tl, RET_DK), lambda hi, li: (li, 0)),
                  pl.BlockSpec((tl, RET_DK), lambda hi, li: (li, 0)),
                  per_head((CHUNK, CHUNK)), per_head((CHUNK, RET_DK)), per_head((CHUNK, RET_DK)),
                  per_head((1, RET_DV))],
        out_specs=pl.BlockSpec((b, tl, vw), lambda hi, li: (0, li, hi)),
        out_shape=jax.ShapeDtypeStruct((b, l, RET_V_W), BF16),
        scratch_shapes=[pltpu.VMEM((b * RET_GROUP, RET_DK, RET_DV), F32)],
        compiler_params=_cparams(("parallel", "arbitrary")),
        name="retention",
    )(proj, proj, proj, proj, cos, sin, intra, qd, kd, cd)


def _hg_kernel(q_ref, f_ref, i_ref, g_ref, lb_ref, o_ref, st_ref, cb_sc, *, tl, nb):
    lblk = pl.program_id(1)

    @pl.when(lblk == 0)
    def _():
        st_ref[...] = jnp.zeros_like(st_ref)

    row = lax.broadcasted_iota(jnp.int32, (CHUNK, HG_DK), 0)
    lane = lax.broadcasted_iota(jnp.int32, (CHUNK, HG_DK), 1)
    tri = (row >= lane).astype(BF16)
    diff_bits = row ^ lane
    level = jnp.where(row == lane, -1, HG_LEVELS)
    for lv in range(HG_LEVELS):
        level = jnp.where((row > lane) & ((diff_bits >> lv) == 1), lv, level)
    half_row = lax.broadcasted_iota(jnp.int32, (8, HG_DK), 0) < 4
    odd_row = (row & 1) == 1
    signed_log2e = [jnp.where((row & (1 << lv)) != 0, LOG2E, -LOG2E) for lv in range(HG_LEVELS)]

    def boundary_decay(cb, cb_ref, lv):
        s = 1 << lv
        if s == 1:
            ref_pt = jnp.where(odd_row, pltpu.roll(cb, 1, 0), cb)
        elif s == 2:
            ref_pt = jnp.concatenate(
                [jnp.where(half_row, jnp.broadcast_to(cb_ref[8 * v + 1:8 * v + 2, :], (8, HG_DK)),
                           jnp.broadcast_to(cb_ref[8 * v + 5:8 * v + 6, :], (8, HG_DK)))
                 for v in range(CHUNK // 8)], axis=0)
        else:
            ref_pt = jnp.concatenate(
                [jnp.broadcast_to(cb_ref[p * 2 * s + s - 1:p * 2 * s + s, :], (2 * s, HG_DK))
                 for p in range(CHUNK // (2 * s))], axis=0)
        return jnp.exp2((cb - ref_pt) * signed_log2e[lv])

    def chunk(bi, hh, r):
        rows = pl.ds(r, CHUNK)
        cols = slice(hh * HG_DK, (hh + 1) * HG_DK)
        ci = bi * HG_GROUP + hh
        cb_ref = cb_sc.at[ci]
        lb = lb_ref[hh]
        log_lb = jnp.log(jnp.maximum(lb, LB_FLOOR))
        log1m_lb = jnp.log1p(-lb)
        one_m_lb = 1.0 - lb
        z = f_ref[bi, rows, cols].astype(F32)
        q = q_ref[bi, rows, cols].astype(F32)
        e = jnp.exp(-jnp.abs(z))
        den = 1.0 + e
        bb = log1m_lb + (jnp.minimum(z, 0.0) - jnp.log(den))
        log_f = jnp.maximum(log_lb, bb) + jnp.log(1.0 + jnp.exp(-jnp.abs(log_lb - bb)))
        k = one_m_lb * (jnp.where(z > 0.0, e, 1.0) / den)
        v = jnp.where(lblk * tl + r + row >= FIRST_VALID, i_ref[bi, rows, cols].astype(F32), 0.0).astype(BF16)
        f_hi = log_f.astype(BF16)
        rest = log_f - f_hi.astype(F32)
        f_mid = rest.astype(BF16)
        f_lo = (rest - f_mid.astype(F32)).astype(BF16)
        cb = _dot(tri, f_hi) + _dot(tri, f_mid) + _dot(tri, f_lo)
        cb_ref[...] = cb
        state = st_ref[ci]
        cend = cb_ref[CHUNK - 1:CHUNK, :]
        o = _dot((q * jnp.exp(cb)).astype(BF16), state.astype(BF16))
        kdec = k * jnp.exp(cend - cb)
        dec_col = jnp.broadcast_to(jnp.exp(cend), (HG_DK, HG_DK)).T
        st_ref[ci] = dec_col * state + _dot(kdec.T.astype(BF16), v)

        amat = jnp.where(level == -1, _dot_nt(q.astype(BF16), k.astype(BF16)), 0.0)
        for lv in range(HG_LEVELS):
            w = boundary_decay(cb, cb_ref, lv)
            amat = jnp.where(level == lv, _dot_nt((q * w).astype(BF16), (k * w).astype(BF16)), amat)
        o = o + _dot(amat.astype(BF16), v)
        o_ref[bi, rows, cols] = (_rms_rows(o) * _silu(g_ref[bi, rows, cols].astype(F32))).astype(o_ref.dtype)

    def body(c, carry):
        r = pl.multiple_of(c * CHUNK, CHUNK)
        for bi in range(nb):
            for hh in range(HG_GROUP):
                chunk(bi, hh, r)
        return carry

    lax.fori_loop(0, tl // CHUNK, body, 0)


def _hgrn2(proj, lb, tl=640):
    b, l, _ = proj.shape
    gw = HG_GROUP * HG_DK

    def col(off):
        return lambda hi, li: (0, li, off // gw + hi)

    return pl.pallas_call(
        functools.partial(_hg_kernel, tl=tl, nb=b),
        grid=(HG_HEADS // HG_GROUP, l // tl),
        in_specs=[pl.BlockSpec((b, tl, gw), col(OFF_HQ)),
                  pl.BlockSpec((b, tl, gw), col(OFF_HF)),
                  pl.BlockSpec((b, tl, gw), col(OFF_HI)),
                  pl.BlockSpec((b, tl, gw), col(OFF_HG)),
                  pl.BlockSpec((HG_GROUP, 1, HG_DK), lambda hi, li: (hi, 0, 0))],
        out_specs=pl.BlockSpec((b, tl, gw), lambda hi, li: (0, li, hi)),
        out_shape=jax.ShapeDtypeStruct((b, l, HG_V_W), BF16),
        scratch_shapes=[pltpu.VMEM((b * HG_GROUP, HG_DK, HG_DV), F32),
                        pltpu.VMEM((b * HG_GROUP, CHUNK, HG_DK), F32)],
        compiler_params=_cparams(("parallel", "arbitrary")),
        name="hgrn2",
    )(proj, proj, proj, proj, lb)


def _da_rope(x, cos, sin):
    lane = lax.broadcasted_iota(jnp.int32, x.shape, 1)
    first_half = (lane % DA_DH) < DA_DH // 2
    swapped = jnp.where(first_half, pltpu.roll(x, LANES - DA_DH // 2, 1), pltpu.roll(x, DA_DH // 2, 1))
    return x * cos + swapped * sin


def _da_prep_kernel(k_ref, cos_ref, sin_ref, ko_ref):
    cos = cos_ref[...]
    sin = sin_ref[...]
    for h in range(DA_HEADS):
        cols = slice(h * LANES, (h + 1) * LANES)
        ko_ref[:, cols] = _da_rope(k_ref[:, cols].astype(F32), cos, sin).astype(BF16)


def _da_prep(proj, cos, sin, tl=1664):
    b, l, _ = proj.shape
    return pl.pallas_call(
        _da_prep_kernel,
        grid=(b, l // tl),
        in_specs=[pl.BlockSpec((None, tl, DA_QK_W), lambda bi, li: (bi, li, OFF_DK // DA_QK_W)),
                  pl.BlockSpec((tl, LANES), lambda bi, li: (li, 0)),
                  pl.BlockSpec((tl, LANES), lambda bi, li: (li, 0))],
        out_specs=pl.BlockSpec((None, tl, DA_QK_W), lambda bi, li: (bi, li, 0)),
        out_shape=jax.ShapeDtypeStruct((b, l, DA_QK_W), BF16),
        compiler_params=_cparams(("parallel", "parallel")),
        name="da_prep",
    )(proj, cos, sin)


def _da_kernel(q_ref, cos_ref, sin_ref, k_ref, v_ref, lam_ref, sg_ref, o_ref, m_ref, l_ref, acc_ref, qm_ref,
               sc_ref, bc_ref, s0_ref, b0_ref, s1_ref, b1_ref, *, tq, nq, lambda_init):
    buf_c = (sc_ref, bc_ref)
    buf_b = ((s0_ref, b0_ref), (s1_ref, b1_ref))
    krow = lax.broadcasted_iota(jnp.int32, (tq, tq), 0)
    qcol = lax.broadcasted_iota(jnp.int32, (tq, tq), 1)
    causal = krow <= qcol
    valid0 = krow >= FIRST_VALID
    maprow = lax.broadcasted_iota(jnp.int32, (2 * DA_DH, tq), 0)
    lp = lam_ref[...]
    lam = (jnp.exp(jnp.sum(lp[0:1] * lp[1:2], axis=1, keepdims=True))
           - jnp.exp(jnp.sum(lp[2:3] * lp[3:4], axis=1, keepdims=True)) + lambda_init)

    def block_rows(i):
        return pl.ds(pl.multiple_of(i * tq, tq), tq)

    def load_queries(qi):
        rows = block_rows(qi)
        q = _da_rope(q_ref[rows, :].astype(F32), cos_ref[rows, :], sin_ref[rows, :]) * (DA_DH ** -0.5 * LOG2E)
        qt = q.T.astype(BF16)
        zero = jnp.zeros_like(qt)
        qm_ref[0] = jnp.where(maprow < DA_DH, qt, zero)
        qm_ref[1] = jnp.where(maprow >= DA_DH, qt, zero)

    def reset():
        m_ref[...] = jnp.full_like(m_ref, MASK_VALUE)
        l_ref[...] = jnp.zeros_like(l_ref)
        acc_ref[...] = jnp.zeros_like(acc_ref)

    def scores(kb, allowed, buf):
        s_ref, bmax_ref = buf
        k = k_ref[block_rows(kb), :]
        for m in range(2):
            s = _dot(k, qm_ref[m])
            if allowed is not None:
                s = jnp.where(allowed, s, MASK_VALUE)
            s_ref[m] = s
            bmax_ref[m] = jnp.max(s, axis=0, keepdims=True)

    def accumulate(kb, buf):
        s_ref, bmax_ref = buf
        v = v_ref[block_rows(kb), :]
        for m in range(2):
            m_prev = m_ref[m]
            m_new = jnp.maximum(m_prev, bmax_ref[m])
            p = jnp.exp2(s_ref[m] - m_new)
            alpha = jnp.exp2(m_prev - m_new)
            l_ref[m] = alpha * l_ref[m] + jnp.sum(p, axis=0, keepdims=True)
            acc_ref[m] = alpha * acc_ref[m] + _dot_tn(v, p.astype(BF16))
            m_ref[m] = m_new

    def finish(qi):
        ot = acc_ref[0] / l_ref[0] - lam * (acc_ref[1] / l_ref[1])
        o_ref[block_rows(qi), :] = (_rms_rows(ot.T) * sg_ref[...] * (1.0 - lambda_init)).astype(o_ref.dtype)
        reset()

    def last_step(qi, buf, more):
        if more:
            load_queries(qi + 1)
            scores(0, valid0, buf_c)
        accumulate(qi, buf)
        finish(qi)

    load_queries(0)
    scores(0, causal & valid0, buf_c)
    reset()

    def query_block(qi, carry):
        @pl.when(qi == 0)
        def _():
            accumulate(0, buf_c)
            finish(0)
            if nq > 1:
                load_queries(1)
                scores(0, valid0, buf_c)

        @pl.when(qi == 1)
        def _():
            scores(1, causal, buf_b[1])
            accumulate(0, buf_c)

        @pl.when(qi >= 2)
        def _():
            scores(1, None, buf_b[1])
            accumulate(0, buf_c)

        n_plain = jnp.maximum(qi - 2, 0)

        def pair(t, c):
            j = 2 * t + 1
            scores(j + 1, None, buf_b[0])
            accumulate(j, buf_b[1])
            scores(j + 2, None, buf_b[1])
            accumulate(j + 1, buf_b[0])
            return c

        lax.fori_loop(0, n_plain // 2, pair, 0)
        odd = lax.rem(qi, 2) == 1

        @pl.when((qi >= 2) & odd)
        def _():
            scores(n_plain + 1, None, buf_b[0])
            accumulate(n_plain, buf_b[1])

        for parity in (0, 1):
            @pl.when((qi >= 2) & (odd == (parity == 1)))
            def _():
                scores(qi, causal, buf_b[parity])
                accumulate(qi - 1, buf_b[1 - parity])

            @pl.when((qi >= 1) & (qi < nq - 1) & (odd == (parity == 1)))
            def _():
                last_step(qi, buf_b[parity], True)

        if nq > 1:
            @pl.when(qi == nq - 1)
            def _():
                last_step(qi, buf_b[(nq - 1) % 2], False)

        return carry

    lax.fori_loop(0, nq, query_block, 0)


def _diff_attn(proj, kr, cos, sin, lam_p, subln_g, lambda_init, tq=640):
    b, l, _ = kr.shape
    once = dict(pipeline_mode=pl.Buffered(1))
    return pl.pallas_call(
        functools.partial(_da_kernel, tq=tq, nq=l // tq, lambda_init=lambda_init),
        grid=(b, DA_HEADS),
        in_specs=[pl.BlockSpec((None, l, LANES), lambda bi, hi: (bi, 0, OFF_DQ // LANES + hi)),
                  pl.BlockSpec((l, LANES), lambda bi, hi: (0, 0), **once),
                  pl.BlockSpec((l, LANES), lambda bi, hi: (0, 0), **once),
                  pl.BlockSpec((None, l, LANES), lambda bi, hi: (bi, 0, hi)),
                  pl.BlockSpec((None, l, DA_DV), lambda bi, hi: (bi, 0, OFF_DV // DA_DV + hi)),
                  pl.BlockSpec((4, DA_DH), lambda bi, hi: (0, 0)),
                  pl.BlockSpec((1, DA_DV), lambda bi, hi: (0, 0))],
        out_specs=pl.BlockSpec((None, l, DA_DV), lambda bi, hi: (bi, 0, hi)),
        out_shape=jax.ShapeDtypeStruct((b, l, DA_V_W), BF16),
        scratch_shapes=[pltpu.VMEM((2, 1, tq), F32),
                        pltpu.VMEM((2, 1, tq), F32),
                        pltpu.VMEM((2, DA_DV, tq), F32),
                        pltpu.VMEM((2, 2 * DA_DH, tq), BF16),
                        pltpu.VMEM((2, tq, tq), F32), pltpu.VMEM((2, 1, tq), F32),
                        pltpu.VMEM((2, tq, tq), F32), pltpu.VMEM((2, 1, tq), F32),
                        pltpu.VMEM((2, tq, tq), F32), pltpu.VMEM((2, 1, tq), F32)],
        compiler_params=_cparams(("parallel", "parallel")),
        name="diff_attn",
    )(proj, cos, sin, kr, proj, lam_p, subln_g)


def _merge_kernel(o0_ref, o1_ref, o2_ref, g0_ref, g1_ref, g2_ref, wb_ref, wo_ref, h_ref, out_ref):
    y = None
    for n, (o_ref, g_ref) in enumerate(((o0_ref, g0_ref), (o1_ref, g1_ref), (o2_ref, g2_ref))):
        term = _sigmoid(g_ref[...].astype(F32)) * _dot(o_ref[...], wb_ref[n])
        y = term if y is None else y + term
    out_ref[...] = h_ref[...] + _dot(y.astype(BF16), wo_ref[...])


def _merge(o_ret, o_hg, o_da, proj2, wb, wo, li, h2, tm=640):
    t, d = h2.shape
    row = lambda i: (i, 0)
    return pl.pallas_call(
        _merge_kernel,
        grid=(t // tm,),
        in_specs=[pl.BlockSpec((tm, d), row), pl.BlockSpec((tm, d), row), pl.BlockSpec((tm, d), row),
                  pl.BlockSpec((tm, d), lambda i: (i, OFF_MG // D_MODEL)),
                  pl.BlockSpec((tm, d), lambda i: (i, OFF_MG // D_MODEL + 1)),
                  pl.BlockSpec((tm, d), lambda i: (i, OFF_MG // D_MODEL + 2)),
                  pl.BlockSpec((None, N_BRANCH, d, d), lambda i: (li, 0, 0, 0), pipeline_mode=pl.Buffered(1)),
                  pl.BlockSpec((None, d, d), lambda i: (li, 0, 0), pipeline_mode=pl.Buffered(1)),
                  pl.BlockSpec((tm, d), row)],
        out_specs=pl.BlockSpec((tm, d), row),
        out_shape=jax.ShapeDtypeStruct((t, d), F32),
        compiler_params=_cparams(("parallel",)),
        name="merge",
    )(o_ret, o_hg, o_da, proj2, proj2, proj2, wb, wo, h2)


FFN_HALO = 8
FFN_FK = 256


def _ffn_kernel(xp_ref, x_ref, g_ref, win_ref, cw_ref, cb_ref, wo_ref, fg_ref, out_ref, act_ref,
                *, tm, seq, final, row0_fn):
    x = jnp.concatenate([xp_ref[...], x_ref[...]], axis=0)
    rows = row0_fn() - FFN_HALO + lax.broadcasted_iota(jnp.int32, (tm + FFN_HALO, 1), 0)
    valid = lax.rem(rows + seq, seq) >= FIRST_VALID
    hn = jnp.where(valid, _rms_rows(x) * g_ref[...], 0.0).astype(BF16)

    def conv(off):
        cols = pl.ds(off, FFN_FK)
        u = _dot(hn, win_ref[:, cols])
        cw = cw_ref[:, cols]
        return (cb_ref[:, cols] + cw[0:1] * pltpu.roll(u, 2, 0)[FFN_HALO:]
                + cw[1:2] * pltpu.roll(u, 1, 0)[FFN_HALO:] + cw[2:3] * u[FFN_HALO:])

    for off in range(0, D_FF, FFN_FK):
        act = _silu(conv(off)) * conv(D_FF + off)
        act_ref[:, pl.ds(off, FFN_FK)] = act.astype(BF16)
    out = x_ref[...] + _dot(act_ref[...], wo_ref[...])
    if final:
        out = _rms_rows(out) * fg_ref[...]
    out_ref[...] = out


def _ffn(h2, g, w_in, conv_w, conv_b, w_out, final_g, li, seq, final, tm=640, tm_final=1024):
    t, d = h2.shape
    if final:
        tm = tm_final
        nt = (seq - PAD) // tm
        grid = (t // seq, nt)
        halo0 = lambda bi, i: bi * (seq // FFN_HALO) + PAD // FFN_HALO + i * (tm // FFN_HALO) - 1
        row0 = lambda bi, i: (halo0(bi, i) + 1) * FFN_HALO
        row0_fn = lambda: row0(pl.program_id(0), pl.program_id(1))
        x_spec = pl.BlockSpec((pl.Element(tm), pl.Element(d)), lambda bi, i: (row0(bi, i), 0))
        halo_spec = pl.BlockSpec((pl.Element(FFN_HALO), pl.Element(d)), lambda bi, i: (halo0(bi, i) * FFN_HALO, 0))
        out_spec = pl.BlockSpec((tm, d), lambda bi, i: (bi * nt + i, 0))
        out_rows = (t // seq) * (seq - PAD)
    else:
        hb = tm // FFN_HALO
        grid = (t // tm,)
        row0_fn = lambda: pl.program_id(0) * tm
        x_spec = pl.BlockSpec((tm, d), lambda i: (i, 0))
        halo_spec = pl.BlockSpec((FFN_HALO, d), lambda i: (jnp.maximum(i * hb - 1, 0), 0))
        out_spec = pl.BlockSpec((tm, d), lambda i: (i, 0))
        out_rows = t
    resident = dict(pipeline_mode=pl.Buffered(1))
    return pl.pallas_call(
        functools.partial(_ffn_kernel, tm=tm, seq=seq, final=final, row0_fn=row0_fn),
        grid=grid,
        in_specs=[halo_spec,
                  x_spec,
                  pl.BlockSpec((None, 1, d), lambda *_: (li, 0, 0)),
                  pl.BlockSpec((None, d, 2 * D_FF), lambda *_: (li, 0, 0), **resident),
                  pl.BlockSpec((None, CONV_W, 2 * D_FF), lambda *_: (li, 0, 0)),
                  pl.BlockSpec((None, 1, 2 * D_FF), lambda *_: (li, 0, 0)),
                  pl.BlockSpec((None, D_FF, d), lambda *_: (li, 0, 0), **resident),
                  pl.BlockSpec((1, d), lambda *_: (0, 0))],
        out_specs=out_spec,
        out_shape=jax.ShapeDtypeStruct((out_rows, d), F32),
        scratch_shapes=[pltpu.VMEM((tm, D_FF), BF16)],
        compiler_params=_cparams(("parallel",) * len(grid)),
        name="ffn",
    )(h2, h2, g, w_in, conv_w, conv_b, w_out, final_g)


def _rope_tables(pos, d, reps):
    inv = ROPE_THETA ** (-jnp.arange(0, d, 2, dtype=F32) / d)
    ang = pos.astype(F32)[:, None] * inv[None, :]
    cos = jnp.cos(ang)
    sin = jnp.sin(ang)
    return jnp.tile(jnp.concatenate([cos, cos], axis=1), (1, reps)), jnp.tile(jnp.concatenate([-sin, sin], axis=1), (1, reps))


def _retention_tables():
    log_g = jnp.log1p(-jnp.exp2(-5.0 - jnp.arange(RET_HEADS, dtype=F32)))
    idx = jnp.arange(CHUNK, dtype=F32)
    gap = idx[:, None] - idx[None, :]
    intra = jnp.where(gap >= 0, jnp.exp(log_g[:, None, None] * jnp.maximum(gap, 0.0)), 0.0)
    q_dec = jnp.exp(log_g[:, None] * (idx[None, :] + 1.0))
    k_dec = jnp.exp(log_g[:, None] * (CHUNK - 1.0 - idx[None, :]))
    c_dec = jnp.exp(log_g * CHUNK)
    qd = jnp.broadcast_to(q_dec[:, :, None], (RET_HEADS, CHUNK, RET_DK))
    kd = jnp.broadcast_to(k_dec[:, :, None], (RET_HEADS, CHUNK, RET_DK))
    cd = jnp.broadcast_to(c_dec[:, None, None], (RET_HEADS, 1, RET_DV))
    return intra, qd, kd, cd


def kernel(x, meta, norm_mix_g, w_in, w_branch, w_out, hg_lb, da_lambda, da_subln_g, norm_ffn_g,
           w_ffn_in, ffn_conv_w, ffn_conv_b, w_ffn_out, norm_final_g):
    b, s, d = x.shape
    l = PAD + s
    h = jnp.concatenate([jnp.zeros((b, PAD - N_META, d), x.dtype),
                         jnp.broadcast_to(meta[None].astype(x.dtype), (b, N_META, d)), x], axis=1)
    pos = jnp.arange(l) - FIRST_VALID
    lb_soft = jax.nn.softmax(hg_lb.astype(F32), axis=0)
    lbs = jnp.cumsum(lb_soft, axis=0) - lb_soft[0]
    ret_cos, ret_sin = _rope_tables(pos, RET_DK, 1)
    da_cos, da_sin = _rope_tables(pos, DA_DH, LANES // DA_DH)
    intra, qd, kd, cd = _retention_tables()

    h2 = h.reshape(b * l, d)
    mix_g = norm_mix_g[:, None, :]
    ffn_g = norm_ffn_g[:, None, :]
    wb_bf, wo_bf = w_branch.astype(BF16), w_out.astype(BF16)
    wfi_bf, wfo_bf = w_ffn_in.astype(BF16), w_ffn_out.astype(BF16)
    conv_b = ffn_conv_b[:, None, :]
    for li in range(DEPTH):
        lambda_init = 0.8 - 0.6 * math.exp(-0.3 * li)
        proj2 = _inproj(h2, mix_g, w_in, li)
        proj = proj2.reshape(b, l, IN_WIDTH)
        o_ret = _retention(proj, ret_cos, ret_sin, intra, qd, kd, cd)
        o_hg = _hgrn2(proj, lbs[li].reshape(HG_HEADS, 1, HG_DK))
        kr = _da_prep(proj, da_cos, da_sin)
        o_da = _diff_attn(proj, kr, da_cos, da_sin, da_lambda[li].astype(F32),
                          da_subln_g[li].astype(F32)[None], lambda_init)
        h2 = _merge(o_ret.reshape(b * l, -1), o_hg.reshape(b * l, -1), o_da.reshape(b * l, -1), proj2,
                    wb_bf, wo_bf, li, h2)
        h2 = _ffn(h2, ffn_g, wfi_bf, ffn_conv_w, conv_b, wfo_bf, norm_final_g[None], li, l,
                  final=li == DEPTH - 1)
    return h2.reshape(b, s, d)
```
